```python
import jax
import jax.numpy as jnp
from jax import lax
import numpy as np

D_MODEL = 2048
BATCH = 4
SEQ = 2048
DEPTH = 4
DEC_BATCH = 8
DEC_SEQ = 4
PAST_LEN = 16384
PAGE_SIZE = 128

N_MIXERS = 3
N_CONV_LAYERS = (DEPTH + 2) // 3
N_DIL_LAYERS = (DEPTH + 1) // 3
N_SB_LAYERS = DEPTH // 3
CONV_WIDTH = 31
DIL_GROUPS = ((128, 1), (512, 4), (2048, 16))
N_DIL_GROUPS = len(DIL_GROUPS)
H_DIL = 8
HD_DIL = 128
H_SB = 16
HD_SB = 128
SB_BIAS_INIT = -6.0
D_FF = ((8 * D_MODEL // 3 + 255) // 256) * 256
Q_BLOCK = 128
EPS = 1e-6
NEG_INF = -1e30
DIL_SCALE = HD_DIL ** -0.5
SB_SCALE = HD_SB ** -0.5

kernel_name = 'hybrid_conv_dilated_stickbreak_decoder_step'


def rms_norm(x, gain):
    x32 = x.astype(jnp.float32)
    y = x32 * lax.rsqrt(jnp.mean(x32 * x32, axis=-1, keepdims=True) + EPS)
    return (y * gain.astype(jnp.float32)).astype(x.dtype)


def layer_norm(x, gain, bias):
    x32 = x.astype(jnp.float32)
    xc = x32 - jnp.mean(x32, axis=-1, keepdims=True)
    y = xc * lax.rsqrt(jnp.mean(xc * xc, axis=-1, keepdims=True) + EPS)
    return (y * gain.astype(jnp.float32) + bias.astype(jnp.float32)).astype(x.dtype)


def swiglu(h, w_gate, w_up, w_down):
    return (jax.nn.silu(h @ w_gate) * (h @ w_up)) @ w_down


def conv_mixer(h, buf, w1, b1, w_dw, b_dw, ln_g, ln_b, w2, b2):
    u = h @ w1 + b1
    g = u[..., :D_MODEL] * jax.nn.sigmoid(u[..., D_MODEL:])
    rows = jnp.concatenate([buf.astype(g.dtype), g], axis=1)
    c = lax.conv_general_dilated(rows, w_dw[:, None, :].astype(g.dtype), window_strides=(1,), padding='VALID',
                                 dimension_numbers=('NWC', 'WIO', 'NWC'), feature_group_count=D_MODEL) + b_dw
    c = layer_norm(c, ln_g, ln_b)
    return jax.nn.silu(c) @ w2 + b2, rows[:, -(CONV_WIDTH - 1):]


def dil_qkv(h, w_qkv, q_gain, k_gain):
    B, T, _ = h.shape
    qkv = (h @ w_qkv).reshape(B, T, 3, N_DIL_GROUPS, H_DIL, HD_DIL)
    q = rms_norm(qkv[:, :, 0], q_gain)
    k = rms_norm(qkv[:, :, 1], k_gain)
    return q, k, qkv[:, :, 2]


def dilated_group_attn(q, k_all, v_all, q_idx, dilation, n_keys):
    idx = q_idx[:, None] - dilation * jnp.arange(n_keys)[None, :]
    valid = idx >= 0
    idx = jnp.maximum(idx, 0)
    kg = jnp.take(k_all, idx, axis=1)
    vg = jnp.take(v_all, idx, axis=1)
    s = jnp.einsum('bqhd,bqkhd->bqhk', q, kg, preferred_element_type=jnp.float32) * DIL_SCALE
    s = jnp.where(valid[None, :, None, :], s, NEG_INF)
    m = jnp.max(s, axis=-1, keepdims=True)
    p = jnp.exp(s - m)
    den = jnp.sum(p, axis=-1)
    o = jnp.einsum('bqhk,bqkhd->bqhd', p, vg.astype(jnp.float32)) / den[..., None]
    return o, m[..., 0] + jnp.log(den)


def combine_groups(outs, lses, w_o, dtype):
    w = jax.nn.softmax(jnp.stack(lses), axis=0)
    o = jnp.einsum('gbth,gbthd->bthd', w, jnp.stack(outs))
    B, T = o.shape[:2]
    return o.reshape(B, T, H_DIL * HD_DIL).astype(dtype) @ w_o


def dil_prompt(h, w_qkv, q_gain, k_gain, w_o):
    B, T, _ = h.shape
    q, k, v = dil_qkv(h, w_qkv, q_gain, k_gain)
    n_blocks = T // Q_BLOCK
    outs, lses, states = [], [], []
    for g, (window, dilation) in enumerate(DIL_GROUPS):
        qg, kg, vg = q[:, :, g], k[:, :, g], v[:, :, g]

        def block(bi):
            b0 = bi * Q_BLOCK
            qb = lax.dynamic_slice_in_dim(qg, b0, Q_BLOCK, axis=1)
            return dilated_group_attn(qb, kg, vg, b0 + jnp.arange(Q_BLOCK), dilation, window // dilation + 1)

        o, lse = lax.map(block, jnp.arange(n_blocks))
        outs.append(jnp.moveaxis(o, 0, 1).reshape(B, T, H_DIL, HD_DIL))
        lses.append(jnp.moveaxis(lse, 0, 1).reshape(B, T, H_DIL))
        states.append(jnp.stack([kg, vg], axis=2)[:, -min(window, T):])
    return combine_groups(outs, lses, w_o, h.dtype), states


def dil_sample(h, bufs, w_qkv, q_gain, k_gain, w_o):
    B, n, _ = h.shape
    q, k, v = dil_qkv(h, w_qkv, q_gain, k_gain)
    outs, lses, states = [], [], []
    for g, (window, dilation) in enumerate(DIL_GROUPS):
        buf = bufs[g].astype(h.dtype)
        L = buf.shape[1]
        rows = jnp.concatenate([buf, jnp.stack([k[:, :, g], v[:, :, g]], axis=2)], axis=1)
        o, lse = dilated_group_attn(q[:, :, g], rows[:, :, 0], rows[:, :, 1], L + jnp.arange(n),
                                    dilation, window // dilation + 1)
        outs.append(o)
        lses.append(lse)
        states.append(rows[:, -L:])
    return combine_groups(outs, lses, w_o, h.dtype), states


def sb_qkv(h, w_qkv):
    B, T, _ = h.shape
    qkv = (h @ w_qkv).reshape(B, T, 3, H_SB, HD_SB)
    return qkv[:, :, 0], qkv[:, :, 1], qkv[:, :, 2]


def sb_block(qh, k, v, bias, mask, surv):
    z = (jnp.einsum('bhqd,bkhd->bhqk', qh, k, preferred_element_type=jnp.float32) * SB_SCALE
         + bias.astype(jnp.float32)[None, :, None, None])
    lneg = jnp.where(mask, jax.nn.log_sigmoid(-z), 0.0)
    between = lax.cumsum(lneg, axis=3, reverse=True) - lneg + surv[..., None]
    a = jnp.where(mask, jnp.exp(jax.nn.log_sigmoid(z) + between), 0.0)
    o = jnp.einsum('bhqk,bkhd->bhqd', a, v.astype(jnp.float32))
    return o, surv + jnp.sum(lneg, axis=-1)


def sb_prompt(h, w_qkv, w_o, bias):
    B, T, _ = h.shape
    q, k, v = sb_qkv(h, w_qkv)
    qh = q.transpose(0, 2, 1, 3)
    outs = []
    for b0 in range(0, T, Q_BLOCK):
        e = b0 + Q_BLOCK
        mask = jnp.arange(e)[None, :] < (b0 + jnp.arange(Q_BLOCK))[:, None]
        o, _ = sb_block(qh[:, :, b0:e], k[:, :e], v[:, :e], bias, mask, jnp.zeros((B, H_SB, Q_BLOCK), jnp.float32))
        outs.append(o)
    o = jnp.concatenate(outs, axis=2)
    y = o.transpose(0, 2, 1, 3).reshape(B, T, H_SB * HD_SB).astype(h.dtype) @ w_o
    return y, jnp.stack([k, v], axis=2)


def sb_sample(h, pool, layer, page_table, w_qkv, w_o, bias):
    B, n, _ = h.shape
    q, k, v = sb_qkv(h, w_qkv)
    qh = q.transpose(0, 2, 1, 3)
    causal = jnp.arange(n)[None, :] < jnp.arange(n)[:, None]
    o, surv = sb_block(qh, k, v, bias, causal, jnp.zeros((B, H_SB, n), jnp.float32))
    full = jnp.ones((n, PAGE_SIZE), dtype=bool)

    def page_step(carry, pages):
        o_acc, surv_acc = carry
        kv = pool[layer, pages].astype(h.dtype)
        o_pg, surv_acc = sb_block(qh, kv[:, :, 0], kv[:, :, 1], bias, full, surv_acc)
        return (o_acc + o_pg, surv_acc), None

    (o, _), _ = lax.scan(page_step, (o, surv), page_table[:, ::-1].T)
    y = o.transpose(0, 2, 1, 3).reshape(B, n, H_SB * HD_SB).astype(h.dtype) @ w_o
    return y, jnp.stack([k, v], axis=2)


def setup_inputs(seed: int = 0) -> dict:
    key = jax.random.key(seed)
    keys = iter(jax.random.split(key, 32))

    def normal(shape, scale=1.0):
        return jax.random.normal(next(keys), shape, jnp.float32) * scale

    n_pages = PAST_LEN // PAGE_SIZE
    n_used = DEC_BATCH * n_pages
    n_pool = n_used + max(1, n_used // 4)
    page_table = jax.random.permutation(next(keys), n_pool)[:n_used].reshape(DEC_BATCH, n_pages).astype(jnp.int32)
    dil_width = N_DIL_GROUPS * H_DIL * HD_DIL
    return {
        'x_prompt': normal((BATCH, SEQ, D_MODEL)),
        'x_sample': normal((DEC_BATCH, DEC_SEQ, D_MODEL)),
        'state_conv': normal((N_CONV_LAYERS, DEC_BATCH, CONV_WIDTH - 1, D_MODEL), 0.5),
        'cache_win_g0': normal((N_DIL_LAYERS, DEC_BATCH, min(DIL_GROUPS[0][0], PAST_LEN), 2, H_DIL, HD_DIL)),
        'cache_win_g1': normal((N_DIL_LAYERS, DEC_BATCH, min(DIL_GROUPS[1][0], PAST_LEN), 2, H_DIL, HD_DIL)),
        'cache_win_g2': normal((N_DIL_LAYERS, DEC_BATCH, min(DIL_GROUPS[2][0], PAST_LEN), 2, H_DIL, HD_DIL)),
        'cache_sb_kv': normal((N_SB_LAYERS, n_pool, PAGE_SIZE, 2, H_SB, HD_SB)),
        'page_table': page_table,
        'ln_mix': 1.0 + normal((DEPTH, D_MODEL), 0.05),
        'ln_ffn': 1.0 + normal((DEPTH, D_MODEL), 0.05),
        'conv_w1': normal((N_CONV_LAYERS, D_MODEL, 2 * D_MODEL), D_MODEL ** -0.5),
        'conv_b1': normal((N_CONV_LAYERS, 2 * D_MODEL), 0.02),
        'conv_w_dw': normal((N_CONV_LAYERS, CONV_WIDTH, D_MODEL), CONV_WIDTH ** -0.5),
        'conv_b_dw': normal((N_CONV_LAYERS, D_MODEL), 0.02),
        'conv_ln_g': 1.0 + normal((N_CONV_LAYERS, D_MODEL), 0.05),
        'conv_ln_b': normal((N_CONV_LAYERS, D_MODEL), 0.02),
        'conv_w2': normal((N_CONV_LAYERS, D_MODEL, D_MODEL), D_MODEL ** -0.5),
        'conv_b2': normal((N_CONV_LAYERS, D_MODEL), 0.02),
        'dil_w_qkv': normal((N_DIL_LAYERS, D_MODEL, 3 * dil_width), D_MODEL ** -0.5),
        'dil_q_gain': 1.0 + normal((N_DIL_LAYERS, HD_DIL), 0.05),
        'dil_k_gain': 1.0 + normal((N_DIL_LAYERS, HD_DIL), 0.05),
        'dil_w_o': normal((N_DIL_LAYERS, H_DIL * HD_DIL, D_MODEL), (H_DIL * HD_DIL) ** -0.5),
        'sb_w_qkv': normal((N_SB_LAYERS, D_MODEL, 3 * H_SB * HD_SB), D_MODEL ** -0.5),
        'sb_w_o': normal((N_SB_LAYERS, H_SB * HD_SB, D_MODEL), (H_SB * HD_SB) ** -0.5),
        'sb_bias': SB_BIAS_INIT + normal((N_SB_LAYERS, H_SB), 0.1),
        'ffn_w_gate': normal((DEPTH, D_MODEL, D_FF), D_MODEL ** -0.5),
        'ffn_w_up': normal((DEPTH, D_MODEL, D_FF), D_MODEL ** -0.5),
        'ffn_w_down': normal((DEPTH, D_FF, D_MODEL), D_FF ** -0.5),
    }


def reference(x_prompt, x_sample, state_conv, cache_win_g0, cache_win_g1, cache_win_g2, cache_sb_kv, page_table,
              ln_mix, ln_ffn, conv_w1, conv_b1, conv_w_dw, conv_b_dw, conv_ln_g, conv_ln_b, conv_w2, conv_b2,
              dil_w_qkv, dil_q_gain, dil_k_gain, dil_w_o, sb_w_qkv, sb_w_o, sb_bias, ffn_w_gate, ffn_w_up, ffn_w_down):
    xp, xs = x_prompt, x_sample
    win_caches = (cache_win_g0, cache_win_g1, cache_win_g2)
    conv_p, conv_s, sb_p, sb_s = [], [], [], []
    win_p = [[] for _ in DIL_GROUPS]
    win_s = [[] for _ in DIL_GROUPS]
    i_conv = i_dil = i_sb = 0
    for i in range(DEPTH):
        hp = rms_norm(xp, ln_mix[i])
        hs = rms_norm(xs, ln_mix[i])
        kind = i % N_MIXERS
        if kind == 0:
            j = i_conv
            prm = (conv_w1[j], conv_b1[j], conv_w_dw[j], conv_b_dw[j], conv_ln_g[j], conv_ln_b[j], conv_w2[j], conv_b2[j])
            zero_buf = jnp.zeros((xp.shape[0], CONV_WIDTH - 1, D_MODEL), xp.dtype)
            mp, st_p = conv_mixer(hp, zero_buf, *prm)
            ms, st_s = conv_mixer(hs, state_conv[j], *prm)
            conv_p.append(st_p)
            conv_s.append(st_s)
            i_conv += 1
        elif kind == 1:
            j = i_dil
            prm = (dil_w_qkv[j], dil_q_gain[j], dil_k_gain[j], dil_w_o[j])
            mp, sts_p = dil_prompt(hp, *prm)
            ms, sts_s = dil_sample(hs, [c[j] for c in win_caches], *prm)
            for g in range(N_DIL_GROUPS):
                win_p[g].append(sts_p[g])
                win_s[g].append(sts_s[g])
            i_dil += 1
        else:
            j = i_sb
            mp, st_p = sb_prompt(hp, sb_w_qkv[j], sb_w_o[j], sb_bias[j])
            ms, st_s = sb_sample(hs, cache_sb_kv, j, page_table, sb_w_qkv[j], sb_w_o[j], sb_bias[j])
            sb_p.append(st_p)
            sb_s.append(st_s)
            i_sb += 1
        xp = xp + mp
        xs = xs + ms
        xp = xp + swiglu(rms_norm(xp, ln_ffn[i]), ffn_w_gate[i], ffn_w_up[i], ffn_w_down[i])
        xs = xs + swiglu(rms_norm(xs, ln_ffn[i]), ffn_w_gate[i], ffn_w_up[i], ffn_w_down[i])
    return (xp, xs, jnp.stack(conv_p), jnp.stack(conv_s),
            jnp.stack(win_p[0]), jnp.stack(win_s[0]), jnp.stack(win_p[1]), jnp.stack(win_s[1]),
            jnp.stack(win_p[2]), jnp.stack(win_s[2]), jnp.stack(sb_p), jnp.stack(sb_s))
```

```python
import functools

import jax
import jax.numpy as jnp
from jax import lax
from jax.experimental import pallas as pl
from jax.experimental.pallas import tpu as pltpu

F32 = jnp.float32
BF16 = jnp.bfloat16

D_MODEL = 2048
CONV_WIDTH = 31
CONV_HIST = CONV_WIDTH - 1
CONV_HALO = 32
DIL_GROUPS = ((128, 1), (512, 4), (2048, 16))
H_DIL = 8
HD = 128
H_SB = 16
PAGE_SIZE = 128
EPS = 1e-6
NEG_INF = -1e30
DIL_SCALE = HD ** -0.5
SB_SCALE = HD ** -0.5

VMEM_LIMIT_BYTES = 56 * 1024 * 1024


def _cparams(sem):
    return pltpu.CompilerParams(dimension_semantics=sem, vmem_limit_bytes=VMEM_LIMIT_BYTES)


def _log_sigmoid(z):
    return jnp.minimum(z, 0.0) - jnp.log1p(jnp.exp(-jnp.abs(z)))


def _rmsnorm_to(xn_ref, x_ref, g_ref):
    x = x_ref[...]
    ms = jnp.mean(x * x, axis=-1, keepdims=True)
    xn_ref[...] = (x * lax.rsqrt(ms + EPS) * g_ref[...]).astype(BF16)


def _norm_dual_kernel(x_ref, g_ref, w0_ref, w1_ref, b0_ref, b1_ref, o_ref, xn_ref, *, act):
    @pl.when(pl.program_id(1) == 0)
    def _():
        _rmsnorm_to(xn_ref, x_ref, g_ref)

    a = xn_ref[...]
    u0 = jnp.dot(a, w0_ref[...].astype(BF16), preferred_element_type=F32) + b0_ref[...]
    u1 = jnp.dot(a, w1_ref[...].astype(BF16), preferred_element_type=F32) + b1_ref[...]
    if act == "swiglu":
        o_ref[...] = (u0 * jax.nn.sigmoid(u0) * u1).astype(o_ref.dtype)
    else:
        o_ref[...] = (u0 * jax.nn.sigmoid(u1)).astype(o_ref.dtype)


def _norm_dual(x, gain, w0, w1, b0, b1, *, act, col0, col1, n_out, out_dtype, tm, tn):
    m, k = x.shape
    o0, o1 = col0 // tn, col1 // tn
    return pl.pallas_call(
        functools.partial(_norm_dual_kernel, act=act),
        grid=(m // tm, n_out // tn),
        in_specs=[
            pl.BlockSpec((tm, k), lambda i, j: (i, 0)),
            pl.BlockSpec((1, k), lambda i, j: (0, 0)),
            pl.BlockSpec((k, tn), lambda i, j: (0, j + o0)),
            pl.BlockSpec((k, tn), lambda i, j: (0, j + o1)),
            pl.BlockSpec((1, tn), lambda i, j: (0, j + o0)),
            pl.BlockSpec((1, tn), lambda i, j: (0, j + o1)),
        ],
        out_specs=pl.BlockSpec((tm, tn), lambda i, j: (i, j)),
        out_shape=jax.ShapeDtypeStruct((m, n_out), out_dtype),
        scratch_shapes=[pltpu.VMEM((tm, k), BF16)],
        compiler_params=_cparams(("parallel", "arbitrary")),
        name=f"norm_dual_{act}",
    )(x, gain.reshape(1, k), w0, w1, b0, b1)


def _norm_single_kernel(x_ref, g_ref, w_ref, hg_ref, o_ref, xn_ref, *, n_normed_tiles):
    j = pl.program_id(1)

    @pl.when(j == 0)
    def _():
        _rmsnorm_to(xn_ref, x_ref, g_ref)

    acc = jnp.dot(xn_ref[...], w_ref[...].astype(BF16), preferred_element_type=F32)

    @pl.when(j < n_normed_tiles)
    def _():
        hg = hg_ref[0]
        for c in range(acc.shape[1] // HD):
            blk = acc[:, c * HD:(c + 1) * HD]
            ms = jnp.mean(blk * blk, axis=-1, keepdims=True)
            o_ref[:, c * HD:(c + 1) * HD] = blk * lax.rsqrt(ms + EPS) * hg

    @pl.when(j >= n_normed_tiles)
    def _():
        o_ref[...] = acc


def _norm_single(x, gain, w, head_gains, *, n_normed_cols, tm, tn):
    m, k = x.shape
    n = w.shape[1]
    n_normed_tiles = n_normed_cols // tn
    n_sec = head_gains.shape[0]
    tiles_per_sec = max(n_normed_tiles // n_sec, 1)
    return pl.pallas_call(
        functools.partial(_norm_single_kernel, n_normed_tiles=n_normed_tiles),
        grid=(m // tm, n // tn),
        in_specs=[
            pl.BlockSpec((tm, k), lambda i, j: (i, 0)),
            pl.BlockSpec((1, k), lambda i, j: (0, 0)),
            pl.BlockSpec((k, tn), lambda i, j: (0, j)),
            pl.BlockSpec((1, 1, HD), lambda i, j: (jnp.minimum(j // tiles_per_sec, n_sec - 1), 0, 0)),
        ],
        out_specs=pl.BlockSpec((tm, tn), lambda i, j: (i, j)),
        out_shape=jax.ShapeDtypeStruct((m, n), F32),
        scratch_shapes=[pltpu.VMEM((tm, k), BF16)],
        compiler_params=_cparams(("parallel", "arbitrary")),
        name="norm_single",
    )(x, gain.reshape(1, k), w, head_gains.reshape(n_sec, 1, HD))


def _res_kernel(a_ref, w_ref, b_ref, r_ref, o_ref):
    acc = jnp.dot(a_ref[...], w_ref[...].astype(BF16), preferred_element_type=F32)
    o_ref[...] = r_ref[...] + (acc + b_ref[...])


def _res_matmul(a, w, bias, res, *, tm, tn):
    m, k = a.shape
    n = w.shape[1]
    return pl.pallas_call(
        _res_kernel,
        grid=(m // tm, n // tn),
        in_specs=[
            pl.BlockSpec((tm, k), lambda i, j: (i, 0)),
            pl.BlockSpec((k, tn), lambda i, j: (0, j)),
            pl.BlockSpec((1, tn), lambda i, j: (0, j)),
            pl.BlockSpec((tm, tn), lambda i, j: (i, j)),
        ],
        out_specs=pl.BlockSpec((tm, tn), lambda i, j: (i, j)),
        out_shape=jax.ShapeDtypeStruct((m, n), F32),
        compiler_params=_cparams(("parallel", "arbitrary")),
        name="res_matmul",
    )(a, w, bias.reshape(1, n), res)


def _conv_kernel(hist_ref, prev_ref, cur_ref, w_ref, bdw_ref, lg_ref, lb_ref, o_ref, win_ref, c_ref,
                 *, tt, rows, cols):
    d = cur_ref.shape[1]
    first = pl.program_id(1) == 0
    win_ref[0:CONV_HALO, :] = jnp.where(first, hist_ref[0], prev_ref[...])
    win_ref[CONV_HALO:CONV_HALO + tt, :] = cur_ref[...]
    off = CONV_HALO - CONV_HIST
    for c0 in range(0, d, cols):
        for r0 in range(0, tt, rows):
            acc = jnp.zeros((rows, cols), F32)
            for k in range(CONV_WIDTH):
                acc = acc + win_ref[r0 + off + k:r0 + off + k + rows, c0:c0 + cols] * w_ref[k:k + 1, c0:c0 + cols]
            c_ref[r0:r0 + rows, c0:c0 + cols] = acc + bdw_ref[:, c0:c0 + cols]
    c = c_ref[...]
    xc = c - jnp.mean(c, axis=-1, keepdims=True)
    y = xc * lax.rsqrt(jnp.mean(xc * xc, axis=-1, keepdims=True) + EPS) * lg_ref[...] + lb_ref[...]
    o_ref[...] = (y * jax.nn.sigmoid(y)).astype(o_ref.dtype)


def _conv_core(g, hist, w_dw, b_dw, ln_g, ln_b, *, n_batch, tt):
    m, d = g.shape
    t = m // n_batch
    nt = t // tt
    ratio = tt // CONV_HALO if tt >= CONV_HALO else None
    if ratio is None:
        prev_spec = pl.BlockSpec((CONV_HALO, d), lambda b, i: (0, 0))
        prev = hist.reshape(n_batch * CONV_HALO, d)
    else:
        prev_spec = pl.BlockSpec((CONV_HALO, d), lambda b, i: (jnp.maximum((b * nt + i) * ratio - 1, 0), 0))
        prev = g
    w_pad = jnp.pad(w_dw, ((0, 32 - CONV_WIDTH), (0, 0)))
    return pl.pallas_call(
        functools.partial(_conv_kernel, tt=tt, rows=min(tt, 32), cols=min(d, 512)),
        grid=(n_batch, nt),
        in_specs=[
            pl.BlockSpec((1, CONV_HALO, d), lambda b, i: (b, 0, 0)),
            prev_spec,
            pl.BlockSpec((tt, d), lambda b, i: (b * nt + i, 0)),
            pl.BlockSpec((32, d), lambda b, i: (0, 0)),
            pl.BlockSpec((1, d), lambda b, i: (0, 0)),
            pl.BlockSpec((1, d), lambda b, i: (0, 0)),
            pl.BlockSpec((1, d), lambda b, i: (0, 0)),
        ],
        out_specs=pl.BlockSpec((tt, d), lambda b, i: (b * nt + i, 0)),
        out_shape=jax.ShapeDtypeStruct((m, d), BF16),
        scratch_shapes=[pltpu.VMEM((CONV_HALO + tt, d), F32), pltpu.VMEM((tt, d), F32)],
        compiler_params=_cparams(("parallel", "arbitrary")),
        name="conv_core",
    )(hist, prev, g, w_pad, b_dw.reshape(1, d), ln_g.reshape(1, d), ln_b.reshape(1, d))


def _softmax_block(s, mask):
    s = jnp.where(mask, s, NEG_INF)
    m = jnp.max(s, axis=-1, keepdims=True)
    p = jnp.exp(s - m)
    den = jnp.sum(p, axis=-1, keepdims=True)
    return p, den, m + jnp.log(den)


def _mix_groups(outs, lses):
    m = jnp.maximum(jnp.maximum(lses[0], lses[1]), lses[2])
    es = [jnp.exp(l - m) for l in lses]
    tot = es[0] + es[1] + es[2]
    return (es[0] * outs[0] + es[1] * outs[1] + es[2] * outs[2]) / tot


def _dil_prompt_kernel(q0, q1, q2, k0, k1, k2, v0, v1, v2, o_ref, og_ref, lse_ref, *, t, qb):
    q_refs, k_refs, v_refs = (q0, q1, q2), (k0, k1, k2), (v0, v1, v2)
    row = lax.broadcasted_iota(jnp.int32, (qb, 2 * qb), 0)
    col = lax.broadcasted_iota(jnp.int32, (qb, 2 * qb), 1)
    band = (col >= row) & (col <= row + qb)
    causal = (lax.broadcasted_iota(jnp.int32, (qb, qb), 1)
              <= lax.broadcasted_iota(jnp.int32, (qb, qb), 0))

    def rows(start, size, d):
        return pl.ds(start, size) if d == 1 else pl.ds(start, size, stride=d)

    for g, (window, d) in enumerate(DIL_GROUPS):
        assert window == qb * d, "a query block plus the previous one must cover the window"
        q_ref, k_ref, v_ref = q_refs[g], k_refs[g], v_refs[g]
        nblk = t // (d * qb)

        def attend(q_start, k_start, nk, mask, g=g, d=d, q_ref=q_ref, k_ref=k_ref, v_ref=v_ref):
            qv = q_ref[0, rows(q_start, qb, d), :].astype(BF16)
            kv = k_ref[0, rows(k_start, nk, d), :].astype(BF16)
            vv = v_ref[0, rows(k_start, nk, d), :].astype(BF16)
            s = lax.dot_general(qv, kv, (((1,), (1,)), ((), ())), preferred_element_type=F32) * DIL_SCALE
            p, den, lse = _softmax_block(s, mask)
            o = jnp.dot(p.astype(BF16), vv, preferred_element_type=F32) / den
            og_ref[g, rows(q_start, qb, d), :] = o
            lse_ref[g, rows(q_start, qb, d), :] = jnp.broadcast_to(lse, (qb, HD))

        def residue(r, carry, d=d, nblk=nblk, attend=attend):
            attend(r, r, qb, causal)

            def later(i, c):
                attend(r + i * qb * d, r + (i - 1) * qb * d, 2 * qb, band)
                return c

            lax.fori_loop(1, nblk, later, 0)
            return carry

        lax.fori_loop(0, d, residue, 0)

    outs = [og_ref[g] for g in range(3)]
    lses = [lse_ref[g] for g in range(3)]
    o_ref[0] = _mix_groups(outs, lses).astype(o_ref.dtype)


def _dil_prompt(qkv, *, n_batch):
    m, n = qkv.shape
    t = m // n_batch
    qkv3 = qkv.reshape(n_batch, t, n)
    nh = len(DIL_GROUPS) * H_DIL

    def spec(sec, g):
        return pl.BlockSpec((1, t, HD), lambda b, h: (b, 0, sec * nh + g * H_DIL + h))

    out = pl.pallas_call(
        functools.partial(_dil_prompt_kernel, t=t, qb=128),
        grid=(n_batch, H_DIL),
        in_specs=[spec(sec, g) for sec in range(3) for g in range(3)],
        out_specs=pl.BlockSpec((1, t, HD), lambda b, h: (b, 0, h)),
        out_shape=jax.ShapeDtypeStruct((n_batch, t, H_DIL * HD), BF16),
        scratch_shapes=[pltpu.VMEM((3, t, HD), F32), pltpu.VMEM((3, t, HD), F32)],
        compiler_params=_cparams(("parallel", "parallel")),
        name="dil_prompt",
    )(*([qkv3] * 9))
    return out.reshape(m, H_DIL * HD)


def _dil_sample_kernel(q0, q1, q2, k0, k1, k2, v0, v1, v2, ck0, ck1, ck2, cv0, cv1, cv2, o_ref, *, n_new):
    q_refs, k_refs, v_refs = (q0, q1, q2), (k0, k1, k2), (v0, v1, v2)
    ck_refs, cv_refs = (ck0, ck1, ck2), (cv0, cv1, cv2)
    nq = q0.shape[1]
    outs, lses = [], []
    for g, (window, d) in enumerate(DIL_GROUPS):
        length = ck_refs[g].shape[1]
        assert length == window and length % d == 0
        qf = q_refs[g][0]
        kc = ck_refs[g][0].astype(BF16)
        vc = cv_refs[g][0].astype(BF16)
        s = lax.dot_general(qf.astype(BF16), kc, (((1,), (1,)), ((), ())), preferred_element_type=F32) * DIL_SCALE
        n_i = lax.broadcasted_iota(jnp.int32, (nq, length), 0)
        c_i = lax.broadcasted_iota(jnp.int32, (nq, length), 1)
        mask = (c_i >= n_i) & (((c_i - n_i) & (d - 1)) == 0)
        s = jnp.where(mask, s, NEG_INF)
        m = jnp.max(s, axis=-1, keepdims=True)
        n_col = lax.broadcasted_iota(jnp.int32, (nq, 1), 0)
        new_scores = []
        for j in range(n_new):
            sj = jnp.sum(qf * k_refs[g][0, j:j + 1, :], axis=-1, keepdims=True) * DIL_SCALE
            sj = jnp.where((n_col >= j) & (((n_col - j) & (d - 1)) == 0), sj, NEG_INF)
            new_scores.append(sj)
            m = jnp.maximum(m, sj)
        p = jnp.exp(s - m)
        den = jnp.sum(p, axis=-1, keepdims=True)
        o = jnp.dot(p.astype(BF16), vc, preferred_element_type=F32)
        for j in range(n_new):
            pj = jnp.exp(new_scores[j] - m)
            den = den + pj
            o = o + pj * v_refs[g][0, j:j + 1, :]
        outs.append(o / den)
        lses.append(m + jnp.log(den))
    o_ref[0] = _mix_groups(outs, lses).astype(o_ref.dtype)


def _dil_sample(qkv, caches, *, n_batch, n_new):
    n = qkv.shape[1]
    nq = 8
    qkv3 = jnp.pad(qkv.reshape(n_batch, n_new, n), ((0, 0), (0, nq - n_new), (0, 0)))
    nh = len(DIL_GROUPS) * H_DIL

    def spec(sec, g):
        return pl.BlockSpec((1, nq, HD), lambda b, h: (b, 0, sec * nh + g * H_DIL + h))

    flat = [c.reshape(n_batch, c.shape[1], 2 * H_DIL * HD) for c in caches]
    ck_specs = [pl.BlockSpec((1, c.shape[1], HD), lambda b, h: (b, 0, h)) for c in flat]
    cv_specs = [pl.BlockSpec((1, c.shape[1], HD), lambda b, h: (b, 0, H_DIL + h)) for c in flat]
    out = pl.pallas_call(
        functools.partial(_dil_sample_kernel, n_new=n_new),
        grid=(n_batch, H_DIL),
        in_specs=[spec(sec, g) for sec in range(3) for g in range(3)] + ck_specs + cv_specs,
        out_specs=pl.BlockSpec((1, nq, HD), lambda b, h: (b, 0, h)),
        out_shape=jax.ShapeDtypeStruct((n_batch, nq, H_DIL * HD), BF16),
        compiler_params=_cparams(("parallel", "parallel")),
        name="dil_sample",
    )(*([qkv3] * 9), *flat, *flat)
    return out[:, :n_new].reshape(n_batch * n_new, H_DIL * HD)


def _split_bf16(x):
    hi = x.astype(BF16)
    lo = (x - hi.astype(F32)).astype(BF16)
    return hi, lo


def _sb_prompt_kernel(bias_ref, q_ref, k_ref, v_ref, o_ref, *, qb):
    h = pl.program_id(1)
    i = pl.program_id(2)
    bias = bias_ref[0, h]
    qv = q_ref[0].astype(BF16)
    row = lax.broadcasted_iota(jnp.int32, (qb, qb), 0)
    col = lax.broadcasted_iota(jnp.int32, (qb, qb), 1)
    later = (row > col).astype(BF16)
    strict = col < row

    def block(k_start, mask, surv, acc):
        kv = k_ref[0, pl.ds(k_start, qb), :].astype(BF16)
        vv = v_ref[0, pl.ds(k_start, qb), :].astype(BF16)
        z = lax.dot_general(qv, kv, (((1,), (1,)), ((), ())), preferred_element_type=F32) * SB_SCALE + bias
        ls = _log_sigmoid(z)
        lneg = ls - z
        if mask is not None:
            lneg = jnp.where(mask, lneg, 0.0)
        hi, lo = _split_bf16(lneg)
        between = (jnp.dot(hi, later, preferred_element_type=F32)
                   + jnp.dot(lo, later, preferred_element_type=F32))
        a = jnp.exp(ls + between + surv)
        if mask is not None:
            a = jnp.where(mask, a, 0.0)
        acc = acc + jnp.dot(a.astype(BF16), vv, preferred_element_type=F32)
        return surv + jnp.sum(lneg, axis=-1, keepdims=True), acc

    surv, acc = block(pl.multiple_of(i * qb, qb), strict, jnp.zeros((qb, 1), F32), jnp.zeros((qb, HD), F32))

    def older(step, carry):
        return block(pl.multiple_of((i - 1 - step) * qb, qb), None, *carry)

    surv, acc = lax.fori_loop(0, i, older, (surv, acc))
    o_ref[0] = acc.astype(o_ref.dtype)


def _sb_prompt(qkv, bias, *, n_batch, qb):
    m, n = qkv.shape
    t = m // n_batch
    qkv3 = qkv.reshape(n_batch, t, n)
    out = pl.pallas_call(
        functools.partial(_sb_prompt_kernel, qb=qb),
        grid=(n_batch, H_SB, t // qb),
        in_specs=[
            pl.BlockSpec(memory_space=pltpu.SMEM),
            pl.BlockSpec((1, qb, HD), lambda b, h, i: (b, i, h)),
            pl.BlockSpec((1, t, HD), lambda b, h, i: (b, 0, H_SB + h)),
            pl.BlockSpec((1, t, HD), lambda b, h, i: (b, 0, 2 * H_SB + h)),
        ],
        out_specs=pl.BlockSpec((1, qb, HD), lambda b, h, i: (b, i, h)),
        out_shape=jax.ShapeDtypeStruct((n_batch, t, H_SB * HD), BF16),
        compiler_params=_cparams(("parallel", "parallel", "arbitrary")),
        name="sb_prompt",
    )(bias.reshape(1, H_SB), qkv3, qkv3, qkv3)
    return out.reshape(m, H_SB * HD)


def _sb_sample_kernel(pt_ref, qbd_ref, brow_ref, new_ref, *rest, n_new, pages_per_step):
    page_refs = rest[:pages_per_step]
    o_ref, acc_ref, surv_ref = rest[pages_per_step:]
    p = pl.program_id(1)
    hw = H_SB * HD
    qbd = qbd_ref[0]
    brow = brow_ref[...]
    key_i = lax.broadcasted_iota(jnp.int32, (PAGE_SIZE, PAGE_SIZE), 0)
    key_j = lax.broadcasted_iota(jnp.int32, (PAGE_SIZE, PAGE_SIZE), 1)
    later = (key_j > key_i).astype(BF16)

    def page(kv, mask, surv):
        kk = kv[:, :hw].astype(BF16)
        vv = kv[:, hw:].astype(BF16)
        z = jnp.dot(kk, qbd, preferred_element_type=F32) * SB_SCALE + brow
        ls = _log_sigmoid(z)
        lneg = ls - z
        if mask is not None:
            lneg = jnp.where(mask, lneg, 0.0)
        hi, lo = _split_bf16(lneg)
        between = (jnp.dot(later, hi, preferred_element_type=F32)
                   + jnp.dot(later, lo, preferred_element_type=F32))
        a = jnp.exp(ls + between + surv)
        if mask is not None:
            a = jnp.where(mask, a, 0.0)
        contrib = lax.dot_general(a.astype(BF16), vv, (((0,), (0,)), ((), ())), preferred_element_type=F32)
        return contrib, surv + jnp.sum(lneg, axis=0, keepdims=True)

    @pl.when(p == 0)
    def _():
        q_of_col = lax.broadcasted_iota(jnp.int32, (PAGE_SIZE, PAGE_SIZE), 1) & 7
        mask = (key_i < q_of_col) & (key_i < n_new)
        contrib, surv = page(new_ref[0], mask, jnp.zeros((1, PAGE_SIZE), F32))
        acc_ref[...] = contrib
        surv_ref[...] = surv

    surv = surv_ref[...]
    total = None
    for ref in page_refs:
        contrib, surv = page(ref[0], None, surv)
        total = contrib if total is None else total + contrib
    acc_ref[...] += total
    surv_ref[...] = surv

    @pl.when(p == pl.num_programs(1) - 1)
    def _():
        for h in range(H_SB):
            o_ref[0, :, h * HD:(h + 1) * HD] = acc_ref[h * 8:(h + 1) * 8, h * HD:(h + 1) * HD].astype(o_ref.dtype)


def _sb_sample(qkv, pool, page_table, bias, *, n_batch, n_new, pages_per_step=4):
    hw = H_SB * HD
    n_pages = page_table.shape[1]
    nq = 8
    q = qkv[:, :hw].reshape(n_batch, n_new, H_SB, HD)
    eye = jnp.eye(H_SB, dtype=F32)
    qbd = jnp.einsum("bnhd,hg->bhdgn", q, eye)
    qbd = jnp.pad(qbd, ((0, 0), (0, 0), (0, 0), (0, 0), (0, nq - n_new))).reshape(n_batch, hw, H_SB * nq).astype(BF16)
    brow = jnp.repeat(bias, nq).reshape(1, H_SB * nq)
    new_kv = jnp.pad(qkv[:, hw:].reshape(n_batch, n_new, 2 * hw), ((0, 0), (0, PAGE_SIZE - n_new), (0, 0)))
    pool3 = pool.reshape(pool.shape[0], PAGE_SIZE, 2 * hw)

    def page_spec(j):
        return pl.BlockSpec((1, PAGE_SIZE, 2 * hw),
                            lambda b, p, pt: (pt[b, n_pages - 1 - (p * pages_per_step + j)], 0, 0))

    grid_spec = pltpu.PrefetchScalarGridSpec(
        num_scalar_prefetch=1,
        grid=(n_batch, n_pages // pages_per_step),
        in_specs=[
            pl.BlockSpec((1, hw, H_SB * nq), lambda b, p, pt: (b, 0, 0)),
            pl.BlockSpec((1, H_SB * nq), lambda b, p, pt: (0, 0)),
            pl.BlockSpec((1, PAGE_SIZE, 2 * hw), lambda b, p, pt: (b, 0, 0)),
        ] + [page_spec(j) for j in range(pages_per_step)],
        out_specs=pl.BlockSpec((1, nq, hw), lambda b, p, pt: (b, 0, 0)),
        scratch_shapes=[pltpu.VMEM((H_SB * nq, hw), F32), pltpu.VMEM((1, H_SB * nq), F32)],
    )
    out = pl.pallas_call(
        functools.partial(_sb_sample_kernel, n_new=n_new, pages_per_step=pages_per_step),
        grid_spec=grid_spec,
        out_shape=jax.ShapeDtypeStruct((n_batch, nq, hw), BF16),
        compiler_params=_cparams(("parallel", "arbitrary")),
        name="sb_sample",
    )(page_table, qbd, brow, new_kv, *([pool3] * pages_per_step))
    return out[:, :n_new].reshape(n_batch * n_new, hw)


def _tiles(m):
    return 1024 if m % 1024 == 0 else m


def _ffn(x, gain, w_gate, w_up, w_down):
    m = x.shape[0]
    d_ff = w_gate.shape[1]
    zeros = jnp.zeros((1, d_ff), F32)
    a = _norm_dual(x, gain, w_gate, w_up, zeros, zeros, act="swiglu", col0=0, col1=0, n_out=d_ff,
                   out_dtype=BF16, tm=_tiles(m), tn=512)
    return _res_matmul(a, w_down, jnp.zeros((x.shape[1],), F32), x, tm=_tiles(m), tn=256)


def _conv_layer(x, gain, hist, w1, b1, w_dw, b_dw, ln_g, ln_b, w2, b2, *, n_batch):
    m, d = x.shape
    t = m // n_batch
    b1r = b1.reshape(1, 2 * d)
    g = _norm_dual(x, gain, w1, w1, b1r, b1r, act="glu", col0=0, col1=d, n_out=d, out_dtype=F32,
                   tm=_tiles(m), tn=512)
    if t >= 256:
        tt, g_pad = 256, g
    else:
        tt = 8
        g_pad = jnp.pad(g.reshape(n_batch, t, d), ((0, 0), (0, tt - t), (0, 0))).reshape(n_batch * tt, d)
    hist_pad = jnp.pad(hist, ((0, 0), (CONV_HALO - CONV_HIST, 0), (0, 0)))
    s = _conv_core(g_pad, hist_pad, w_dw, b_dw, ln_g, ln_b, n_batch=n_batch, tt=tt)
    if tt != 256:
        s = s.reshape(n_batch, tt, d)[:, :t].reshape(m, d)
    x = _res_matmul(s, w2, b2, x, tm=_tiles(m), tn=512)
    state = jnp.concatenate([hist, g.reshape(n_batch, t, d)], axis=1)[:, -CONV_HIST:]
    return x, state


def _dil_qkv(x, gain, w_qkv, q_gain, k_gain):
    m = x.shape[0]
    third = w_qkv.shape[1] // 3
    return _norm_single(x, gain, w_qkv, jnp.stack([q_gain, k_gain]), n_normed_cols=2 * third, tm=_tiles(m), tn=512)


def _dil_states(qkv, n_batch):
    t = qkv.shape[0] // n_batch
    q5 = qkv.reshape(n_batch, t, 3, len(DIL_GROUPS), H_DIL, HD)
    return [jnp.stack([q5[:, :, 1, g], q5[:, :, 2, g]], axis=2) for g in range(len(DIL_GROUPS))]


def kernel(x_prompt, x_sample, state_conv, cache_win_g0, cache_win_g1, cache_win_g2, cache_sb_kv, page_table, ln_mix, ln_ffn, conv_w1, conv_b1, conv_w_dw, conv_b_dw, conv_ln_g, conv_ln_b, conv_w2, conv_b2, dil_w_qkv, dil_q_gain, dil_k_gain, dil_w_o, sb_w_qkv, sb_w_o, sb_bias, ffn_w_gate, ffn_w_up, ffn_w_down):
    bp, tp, d = x_prompt.shape
    bs, ts, _ = x_sample.shape
    depth = ln_mix.shape[0]
    xp = x_prompt.reshape(bp * tp, d)
    xs = x_sample.reshape(bs * ts, d)
    win_caches = (cache_win_g0, cache_win_g1, cache_win_g2)
    conv_p, conv_s, sb_p, sb_s = [], [], [], []
    win_p = [[] for _ in DIL_GROUPS]
    win_s = [[] for _ in DIL_GROUPS]
    i_conv = i_dil = i_sb = 0
    zero_d = jnp.zeros((d,), F32)
    for i in range(depth):
        kind = i % 3
        if kind == 0:
            j = i_conv
            prm = (conv_w1[j], conv_b1[j], conv_w_dw[j], conv_b_dw[j], conv_ln_g[j], conv_ln_b[j], conv_w2[j], conv_b2[j])
            xp, st_p = _conv_layer(xp, ln_mix[i], jnp.zeros((bp, CONV_HIST, d), F32), *prm, n_batch=bp)
            xs, st_s = _conv_layer(xs, ln_mix[i], state_conv[j], *prm, n_batch=bs)
            conv_p.append(st_p)
            conv_s.append(st_s)
            i_conv += 1
        elif kind == 1:
            j = i_dil
            qkv_p = _dil_qkv(xp, ln_mix[i], dil_w_qkv[j], dil_q_gain[j], dil_k_gain[j])
            qkv_s = _dil_qkv(xs, ln_mix[i], dil_w_qkv[j], dil_q_gain[j], dil_k_gain[j])
            o_p = _dil_prompt(qkv_p, n_batch=bp)
            o_s = _dil_sample(qkv_s, [c[j] for c in win_caches], n_batch=bs, n_new=ts)
            xp = _res_matmul(o_p, dil_w_o[j], zero_d, xp, tm=_tiles(xp.shape[0]), tn=512)
            xs = _res_matmul(o_s, dil_w_o[j], zero_d, xs, tm=_tiles(xs.shape[0]), tn=512)
            new_p = _dil_states(qkv_p, bp)
            new_s = _dil_states(qkv_s, bs)
            for g, (window, _) in enumerate(DIL_GROUPS):
                win_p[g].append(new_p[g][:, -min(window, tp):])
                length = win_caches[g].shape[2]
                win_s[g].append(jnp.concatenate([win_caches[g][j], new_s[g]], axis=1)[:, -length:])
            i_dil += 1
        else:
            j = i_sb
            hw = H_SB * HD
            ones_gain = jnp.ones((1, HD), F32)
            qkv_p = _norm_single(xp, ln_mix[i], sb_w_qkv[j], ones_gain, n_normed_cols=0, tm=_tiles(xp.shape[0]), tn=512)
            qkv_s = _norm_single(xs, ln_mix[i], sb_w_qkv[j], ones_gain, n_normed_cols=0, tm=_tiles(xs.shape[0]), tn=512)
            o_p = _sb_prompt(qkv_p, sb_bias[j], n_batch=bp, qb=256)
            o_s = _sb_sample(qkv_s, cache_sb_kv[j], page_table, sb_bias[j], n_batch=bs, n_new=ts)
            xp = _res_matmul(o_p, sb_w_o[j], zero_d, xp, tm=_tiles(xp.shape[0]), tn=512)
            xs = _res_matmul(o_s, sb_w_o[j], zero_d, xs, tm=_tiles(xs.shape[0]), tn=512)
            sb_p.append(qkv_p[:, hw:].reshape(bp, tp, 2, H_SB, HD))
            sb_s.append(qkv_s[:, hw:].reshape(bs, ts, 2, H_SB, HD))
            i_sb += 1
        xp = _ffn(xp, ln_ffn[i], ffn_w_gate[i], ffn_w_up[i], ffn_w_down[i])
        xs = _ffn(xs, ln_ffn[i], ffn_w_gate[i], ffn_w_up[i], ffn_w_down[i])
    return (xp.reshape(bp, tp, d), xs.reshape(bs, ts, d), jnp.stack(conv_p), jnp.stack(conv_s),
            jnp.stack(win_p[0]), jnp.stack(win_s[0]), jnp.stack(win_p[1]), jnp.stack(win_s[1]),
            jnp.stack(win_p[2]), jnp.stack(win_s[2]), jnp.stack(sb_p), jnp.stack(sb_s))
```

```python
import functools

import jax
import jax.numpy as jnp
from jax import lax
from jax.experimental import pallas as pl
from jax.experimental.pallas import tpu as pltpu

F32 = jnp.float32
BF16 = jnp.bfloat16

D_MODEL = 2048
CONV_WIDTH = 31
CONV_HIST = CONV_WIDTH - 1
CONV_HALO = 32
SUBLANES = 8
SB_BLOCK_HEADS = 8
SB_Q_SLOTS = 8
DIL_GROUPS = ((128, 1), (512, 4), (2048, 16))
H_DIL = 8
HD = 128
H_SB = 16
PAGE_SIZE = 128
EPS = 1e-6
NEG_INF = -1e30
DIL_SCALE = HD ** -0.5
SB_SCALE = HD ** -0.5

VMEM_LIMIT_BYTES = 56 * 1024 * 1024


def _cparams(sem):
    return pltpu.CompilerParams(dimension_semantics=sem, vmem_limit_bytes=VMEM_LIMIT_BYTES)


def _log_sigmoid(z):
    return jnp.minimum(z, 0.0) - jnp.log1p(jnp.exp(-jnp.abs(z)))


def _rmsnorm_to(xn_ref, x_ref, g_ref):
    x = x_ref[...]
    ms = jnp.mean(x * x, axis=-1, keepdims=True)
    xn_ref[...] = (x * lax.rsqrt(ms + EPS) * g_ref[...]).astype(BF16)


def _norm_dual_kernel(x_ref, g_ref, w0_ref, w1_ref, b0_ref, b1_ref, o_ref, xn_ref, *, act):
    @pl.when(pl.program_id(1) == 0)
    def _():
        _rmsnorm_to(xn_ref, x_ref, g_ref)

    a = xn_ref[...]
    u0 = jnp.dot(a, w0_ref[...].astype(BF16), preferred_element_type=F32) + b0_ref[...]
    u1 = jnp.dot(a, w1_ref[...].astype(BF16), preferred_element_type=F32) + b1_ref[...]
    if act == "swiglu":
        o_ref[...] = (u0 * jax.nn.sigmoid(u0) * u1).astype(o_ref.dtype)
    else:
        o_ref[...] = (u0 * jax.nn.sigmoid(u1)).astype(o_ref.dtype)


def _norm_dual(x, gain, w0, w1, b0, b1, *, act, col0, col1, n_out, out_dtype, tm, tn):
    m, k = x.shape
    o0, o1 = col0 // tn, col1 // tn
    return pl.pallas_call(
        functools.partial(_norm_dual_kernel, act=act),
        grid=(m // tm, n_out // tn),
        in_specs=[
            pl.BlockSpec((tm, k), lambda i, j: (i, 0)),
            pl.BlockSpec((1, k), lambda i, j: (0, 0)),
            pl.BlockSpec((k, tn), lambda i, j: (0, j + o0)),
            pl.BlockSpec((k, tn), lambda i, j: (0, j + o1)),
            pl.BlockSpec((1, tn), lambda i, j: (0, j + o0)),
            pl.BlockSpec((1, tn), lambda i, j: (0, j + o1)),
        ],
        out_specs=pl.BlockSpec((tm, tn), lambda i, j: (i, j)),
        out_shape=jax.ShapeDtypeStruct((m, n_out), out_dtype),
        scratch_shapes=[pltpu.VMEM((tm, k), BF16)],
        compiler_params=_cparams(("parallel", "arbitrary")),
        name=f"norm_dual_{act}",
    )(x, gain.reshape(1, k), w0, w1, b0, b1)


def _norm_single_kernel(x_ref, g_ref, w_ref, hg_ref, o_ref, xn_ref, *, n_normed_tiles):
    j = pl.program_id(1)

    @pl.when(j == 0)
    def _():
        _rmsnorm_to(xn_ref, x_ref, g_ref)

    acc = jnp.dot(xn_ref[...], w_ref[...].astype(BF16), preferred_element_type=F32)

    @pl.when(j < n_normed_tiles)
    def _():
        hg = hg_ref[0]
        for c in range(acc.shape[1] // HD):
            blk = acc[:, c * HD:(c + 1) * HD]
            ms = jnp.mean(blk * blk, axis=-1, keepdims=True)
            o_ref[:, c * HD:(c + 1) * HD] = blk * lax.rsqrt(ms + EPS) * hg

    @pl.when(j >= n_normed_tiles)
    def _():
        o_ref[...] = acc


def _norm_single(x, gain, w, head_gains, *, n_normed_cols, tm, tn):
    m, k = x.shape
    n = w.shape[1]
    n_normed_tiles = n_normed_cols // tn
    n_sec = head_gains.shape[0]
    tiles_per_sec = max(n_normed_tiles // n_sec, 1)
    return pl.pallas_call(
        functools.partial(_norm_single_kernel, n_normed_tiles=n_normed_tiles),
        grid=(m // tm, n // tn),
        in_specs=[
            pl.BlockSpec((tm, k), lambda i, j: (i, 0)),
            pl.BlockSpec((1, k), lambda i, j: (0, 0)),
            pl.BlockSpec((k, tn), lambda i, j: (0, j)),
            pl.BlockSpec((1, 1, HD), lambda i, j: (jnp.minimum(j // tiles_per_sec, n_sec - 1), 0, 0)),
        ],
        out_specs=pl.BlockSpec((tm, tn), lambda i, j: (i, j)),
        out_shape=jax.ShapeDtypeStruct((m, n), F32),
        scratch_shapes=[pltpu.VMEM((tm, k), BF16)],
        compiler_params=_cparams(("parallel", "arbitrary")),
        name="norm_single",
    )(x, gain.reshape(1, k), w, head_gains.reshape(n_sec, 1, HD))


def _res_kernel(a_ref, w_ref, b_ref, r_ref, o_ref):
    acc = jnp.dot(a_ref[...], w_ref[...].astype(BF16), preferred_element_type=F32)
    o_ref[...] = r_ref[...] + (acc + b_ref[...])


def _res_matmul(a, w, bias, res, *, tm, tn):
    m, k = a.shape
    n = w.shape[1]
    return pl.pallas_call(
        _res_kernel,
        grid=(m // tm, n // tn),
        in_specs=[
            pl.BlockSpec((tm, k), lambda i, j: (i, 0)),
            pl.BlockSpec((k, tn), lambda i, j: (0, j)),
            pl.BlockSpec((1, tn), lambda i, j: (0, j)),
            pl.BlockSpec((tm, tn), lambda i, j: (i, j)),
        ],
        out_specs=pl.BlockSpec((tm, tn), lambda i, j: (i, j)),
        out_shape=jax.ShapeDtypeStruct((m, n), F32),
        compiler_params=_cparams(("parallel", "arbitrary")),
        name="res_matmul",
    )(a, w, bias.reshape(1, n), res)


def _conv_kernel(hist_ref, prev_ref, cur_ref, w_ref, bdw_ref, lg_ref, lb_ref, o_ref, win_ref, sh_ref, c_ref,
                 *, tt, rows, cols):
    d = cur_ref.shape[1]
    first = pl.program_id(1) == 0
    win_ref[0:CONV_HALO, :] = jnp.where(first, hist_ref[0], prev_ref[...])
    win_ref[CONV_HALO:CONV_HALO + tt, :] = cur_ref[...]
    off = CONV_HALO - CONV_HIST
    n_sh = CONV_HALO + tt - SUBLANES
    for c0 in range(0, d, cols):
        for s in range(1, SUBLANES):
            sh_ref[s, 0:n_sh, :] = win_ref[s:s + n_sh, c0:c0 + cols]
        for r0 in range(0, tt, rows):
            acc = jnp.zeros((rows, cols), F32)
            for k in range(CONV_WIDTH):
                s = (off + k) % SUBLANES
                a = r0 + off + k - s
                tap = win_ref[a:a + rows, c0:c0 + cols] if s == 0 else sh_ref[s, a:a + rows, :]
                acc = acc + tap * w_ref[k:k + 1, c0:c0 + cols]
            c_ref[r0:r0 + rows, c0:c0 + cols] = acc + bdw_ref[:, c0:c0 + cols]
    c = c_ref[...]
    xc = c - jnp.mean(c, axis=-1, keepdims=True)
    y = xc * lax.rsqrt(jnp.mean(xc * xc, axis=-1, keepdims=True) + EPS) * lg_ref[...] + lb_ref[...]
    o_ref[...] = (y * jax.nn.sigmoid(y)).astype(o_ref.dtype)


def _conv_core(g, hist, w_dw, b_dw, ln_g, ln_b, *, n_batch, tt):
    m, d = g.shape
    t = m // n_batch
    nt = t // tt
    ratio = tt // CONV_HALO if tt >= CONV_HALO else None
    if ratio is None:
        prev_spec = pl.BlockSpec((CONV_HALO, d), lambda b, i: (0, 0))
        prev = hist.reshape(n_batch * CONV_HALO, d)
    else:
        prev_spec = pl.BlockSpec((CONV_HALO, d), lambda b, i: (jnp.maximum((b * nt + i) * ratio - 1, 0), 0))
        prev = g
    w_pad = jnp.pad(w_dw, ((0, 32 - CONV_WIDTH), (0, 0)))
    return pl.pallas_call(
        functools.partial(_conv_kernel, tt=tt, rows=min(tt, 32), cols=min(d, 512)),
        grid=(n_batch, nt),
        in_specs=[
            pl.BlockSpec((1, CONV_HALO, d), lambda b, i: (b, 0, 0)),
            prev_spec,
            pl.BlockSpec((tt, d), lambda b, i: (b * nt + i, 0)),
            pl.BlockSpec((32, d), lambda b, i: (0, 0)),
            pl.BlockSpec((1, d), lambda b, i: (0, 0)),
            pl.BlockSpec((1, d), lambda b, i: (0, 0)),
            pl.BlockSpec((1, d), lambda b, i: (0, 0)),
        ],
        out_specs=pl.BlockSpec((tt, d), lambda b, i: (b * nt + i, 0)),
        out_shape=jax.ShapeDtypeStruct((m, d), BF16),
        scratch_shapes=[pltpu.VMEM((CONV_HALO + tt, d), F32),
                        pltpu.VMEM((SUBLANES, CONV_HALO + tt - SUBLANES, min(d, 512)), F32),
                        pltpu.VMEM((tt, d), F32)],
        compiler_params=_cparams(("parallel", "arbitrary")),
        name="conv_core",
    )(hist, prev, g, w_pad, b_dw.reshape(1, d), ln_g.reshape(1, d), ln_b.reshape(1, d))


def _softmax_block(s, mask):
    s = jnp.where(mask, s, NEG_INF)
    m = jnp.max(s, axis=-1, keepdims=True)
    p = jnp.exp(s - m)
    den = jnp.sum(p, axis=-1, keepdims=True)
    return p, den, m + jnp.log(den)


def _mix_groups(outs, lses):
    m = jnp.maximum(jnp.maximum(lses[0], lses[1]), lses[2])
    es = [jnp.exp(l - m) for l in lses]
    tot = es[0] + es[1] + es[2]
    return (es[0] * outs[0] + es[1] * outs[1] + es[2] * outs[2]) / tot


DIL_UNROLL = 8


def _largest_divisor(n, cap):
    return max(u for u in range(1, cap + 1) if n % u == 0)


def _dil_prompt_kernel(q0, q1, q2, k0, k1, k2, v0, v1, v2, o_ref, og_ref, lse_ref, *, t, qb):
    q_refs, k_refs, v_refs = (q0, q1, q2), (k0, k1, k2), (v0, v1, v2)
    row = lax.broadcasted_iota(jnp.int32, (qb, 2 * qb), 0)
    col = lax.broadcasted_iota(jnp.int32, (qb, 2 * qb), 1)
    band = (col >= row) & (col <= row + qb)
    causal = (lax.broadcasted_iota(jnp.int32, (qb, qb), 1)
              <= lax.broadcasted_iota(jnp.int32, (qb, qb), 0))

    def rows(start, size, d):
        return pl.ds(start, size) if d == 1 else pl.ds(start, size, stride=d)

    for g, (window, d) in enumerate(DIL_GROUPS):
        assert window == qb * d, "a query block plus the previous one must cover the window"
        q_ref, k_ref, v_ref = q_refs[g], k_refs[g], v_refs[g]
        nblk = t // (d * qb)

        def attend(q_start, k_start, nk, mask, g=g, d=d, q_ref=q_ref, k_ref=k_ref, v_ref=v_ref):
            qv = q_ref[0, rows(q_start, qb, d), :].astype(BF16)
            kv = k_ref[0, rows(k_start, nk, d), :].astype(BF16)
            vv = v_ref[0, rows(k_start, nk, d), :].astype(BF16)
            s = lax.dot_general(qv, kv, (((1,), (1,)), ((), ())), preferred_element_type=F32) * DIL_SCALE
            p, den, lse = _softmax_block(s, mask)
            o = jnp.dot(p.astype(BF16), vv, preferred_element_type=F32) / den
            og_ref[g, rows(q_start, qb, d), :] = o
            lse_ref[g, rows(q_start, qb, d), :] = jnp.broadcast_to(lse, (qb, HD))

        def residue(r, carry, d=d, nblk=nblk, attend=attend):
            attend(r, r, qb, causal)

            def later(i, c):
                attend(r + i * qb * d, r + (i - 1) * qb * d, 2 * qb, band)
                return c

            if nblk > 1:
                lax.fori_loop(1, nblk, later, 0, unroll=_largest_divisor(nblk - 1, DIL_UNROLL))
            return carry

        lax.fori_loop(0, d, residue, 0, unroll=_largest_divisor(d, DIL_UNROLL) if nblk == 1 else 1)

    outs = [og_ref[g] for g in range(3)]
    lses = [lse_ref[g] for g in range(3)]
    o_ref[0] = _mix_groups(outs, lses).astype(o_ref.dtype)


def _dil_prompt(qkv, *, n_batch):
    m, n = qkv.shape
    t = m // n_batch
    qkv3 = qkv.reshape(n_batch, t, n)
    nh = len(DIL_GROUPS) * H_DIL

    def spec(sec, g):
        return pl.BlockSpec((1, t, HD), lambda b, h: (b, 0, sec * nh + g * H_DIL + h))

    out = pl.pallas_call(
        functools.partial(_dil_prompt_kernel, t=t, qb=128),
        grid=(n_batch, H_DIL),
        in_specs=[spec(sec, g) for sec in range(3) for g in range(3)],
        out_specs=pl.BlockSpec((1, t, HD), lambda b, h: (b, 0, h)),
        out_shape=jax.ShapeDtypeStruct((n_batch, t, H_DIL * HD), BF16),
        scratch_shapes=[pltpu.VMEM((3, t, HD), F32), pltpu.VMEM((3, t, HD), F32)],
        compiler_params=_cparams(("parallel", "parallel")),
        name="dil_prompt",
    )(*([qkv3] * 9))
    return out.reshape(m, H_DIL * HD)


def _dil_sample_kernel(q0, q1, q2, k0, k1, k2, v0, v1, v2, ck0, ck1, ck2, cv0, cv1, cv2, o_ref, *, n_new):
    q_refs, k_refs, v_refs = (q0, q1, q2), (k0, k1, k2), (v0, v1, v2)
    ck_refs, cv_refs = (ck0, ck1, ck2), (cv0, cv1, cv2)
    nq = q0.shape[1]
    outs, lses = [], []
    for g, (window, d) in enumerate(DIL_GROUPS):
        length = ck_refs[g].shape[1]
        assert length == window and length % d == 0
        qf = q_refs[g][0]
        kc = ck_refs[g][0].astype(BF16)
        vc = cv_refs[g][0].astype(BF16)
        s = lax.dot_general(qf.astype(BF16), kc, (((1,), (1,)), ((), ())), preferred_element_type=F32) * DIL_SCALE
        n_i = lax.broadcasted_iota(jnp.int32, (nq, length), 0)
        c_i = lax.broadcasted_iota(jnp.int32, (nq, length), 1)
        mask = (c_i >= n_i) & (((c_i - n_i) & (d - 1)) == 0)
        s = jnp.where(mask, s, NEG_INF)
        m = jnp.max(s, axis=-1, keepdims=True)
        n_col = lax.broadcasted_iota(jnp.int32, (nq, 1), 0)
        new_scores = []
        for j in range(n_new):
            sj = jnp.sum(qf * k_refs[g][0, j:j + 1, :], axis=-1, keepdims=True) * DIL_SCALE
            sj = jnp.where((n_col >= j) & (((n_col - j) & (d - 1)) == 0), sj, NEG_INF)
            new_scores.append(sj)
            m = jnp.maximum(m, sj)
        p = jnp.exp(s - m)
        den = jnp.sum(p, axis=-1, keepdims=True)
        o = jnp.dot(p.astype(BF16), vc, preferred_element_type=F32)
        for j in range(n_new):
            pj = jnp.exp(new_scores[j] - m)
            den = den + pj
            o = o + pj * v_refs[g][0, j:j + 1, :]
        outs.append(o / den)
        lses.append(m + jnp.log(den))
    o_ref[0] = _mix_groups(outs, lses).astype(o_ref.dtype)


def _dil_sample(qkv, caches, *, n_batch, n_new):
    n = qkv.shape[1]
    nq = 8
    qkv3 = jnp.pad(qkv.reshape(n_batch, n_new, n), ((0, 0), (0, nq - n_new), (0, 0)))
    nh = len(DIL_GROUPS) * H_DIL

    def spec(sec, g):
        return pl.BlockSpec((1, nq, HD), lambda b, h: (b, 0, sec * nh + g * H_DIL + h))

    flat = [c.reshape(n_batch, c.shape[1], 2 * H_DIL * HD) for c in caches]
    ck_specs = [pl.BlockSpec((1, c.shape[1], HD), lambda b, h: (b, 0, h)) for c in flat]
    cv_specs = [pl.BlockSpec((1, c.shape[1], HD), lambda b, h: (b, 0, H_DIL + h)) for c in flat]
    out = pl.pallas_call(
        functools.partial(_dil_sample_kernel, n_new=n_new),
        grid=(n_batch, H_DIL),
        in_specs=[spec(sec, g) for sec in range(3) for g in range(3)] + ck_specs + cv_specs,
        out_specs=pl.BlockSpec((1, nq, HD), lambda b, h: (b, 0, h)),
        out_shape=jax.ShapeDtypeStruct((n_batch, nq, H_DIL * HD), BF16),
        compiler_params=_cparams(("parallel", "parallel")),
        name="dil_sample",
    )(*([qkv3] * 9), *flat, *flat)
    return out[:, :n_new].reshape(n_batch * n_new, H_DIL * HD)


def _split_bf16(x):
    hi = x.astype(BF16)
    lo = (x - hi.astype(F32)).astype(BF16)
    return hi, lo


def _sb_prompt_kernel(bias_ref, q_ref, k_ref, v_ref, o_ref, acc_ref, surv_ref, *, qb, heads):
    hp = pl.program_id(1)
    i = pl.program_id(2)
    row = lax.broadcasted_iota(jnp.int32, (qb, qb), 0)
    col = lax.broadcasted_iota(jnp.int32, (qb, qb), 1)
    later = (row > col).astype(BF16)
    later2 = jnp.concatenate([later, later], axis=0)
    strict = col < row
    lanes = [slice(hh * HD, (hh + 1) * HD) for hh in range(heads)]
    biases = [bias_ref[0, hp * heads + hh] for hh in range(heads)]
    qs = [(q_ref[0, :, lanes[hh]] * SB_SCALE).astype(BF16) for hh in range(heads)]

    def blocks(k_start, mask):
        zs = []
        for hh in range(heads):
            kv = k_ref[0, pl.ds(k_start, qb), lanes[hh]].astype(BF16)
            zs.append(lax.dot_general(qs[hh], kv, (((1,), (1,)), ((), ())), preferred_element_type=F32) + biases[hh])
        lss, lnegs, betweens = [], [], []
        for hh in range(heads):
            z = zs[hh]
            t = jnp.log(1.0 + jnp.exp(-jnp.abs(z)))
            ls = jnp.minimum(z, 0.0) - t
            lneg = ls - z
            if mask is not None:
                lneg = jnp.where(mask, lneg, 0.0)
            hi, lo = _split_bf16(lneg)
            betweens.append(jnp.dot(jnp.concatenate([hi, lo], axis=1), later2, preferred_element_type=F32))
            lss.append(ls)
            lnegs.append(lneg)
        for hh in range(heads):
            surv = surv_ref[hh]
            a = jnp.exp(lss[hh] + betweens[hh] + surv)
            if mask is not None:
                a = jnp.where(mask, a, 0.0)
            vv = v_ref[0, pl.ds(k_start, qb), lanes[hh]].astype(BF16)
            acc_ref[hh] += jnp.dot(a.astype(BF16), vv, preferred_element_type=F32)
            surv_ref[hh] = surv + jnp.sum(lnegs[hh], axis=-1, keepdims=True)

    acc_ref[...] = jnp.zeros(acc_ref.shape, F32)
    surv_ref[...] = jnp.zeros(surv_ref.shape, F32)
    blocks(pl.multiple_of(i * qb, qb), strict)

    def older(step, carry):
        blocks(pl.multiple_of((i - 1 - step) * qb, qb), None)
        return carry

    lax.fori_loop(0, i, older, 0)
    for hh in range(heads):
        o_ref[0, :, lanes[hh]] = acc_ref[hh].astype(o_ref.dtype)


def _sb_prompt(qkv, bias, *, n_batch, qb, heads=8):
    m, n = qkv.shape
    t = m // n_batch
    qkv3 = qkv.reshape(n_batch, t, n)
    hg = H_SB // heads
    w = heads * HD
    out = pl.pallas_call(
        functools.partial(_sb_prompt_kernel, qb=qb, heads=heads),
        grid=(n_batch, hg, t // qb),
        in_specs=[
            pl.BlockSpec(memory_space=pltpu.SMEM),
            pl.BlockSpec((1, qb, w), lambda b, h, i: (b, i, h)),
            pl.BlockSpec((1, t, w), lambda b, h, i: (b, 0, hg + h)),
            pl.BlockSpec((1, t, w), lambda b, h, i: (b, 0, 2 * hg + h)),
        ],
        out_specs=pl.BlockSpec((1, qb, w), lambda b, h, i: (b, i, h)),
        out_shape=jax.ShapeDtypeStruct((n_batch, t, H_SB * HD), BF16),
        scratch_shapes=[pltpu.VMEM((heads, qb, HD), F32), pltpu.VMEM((heads, qb, 1), F32)],
        compiler_params=_cparams(("parallel", "parallel", "arbitrary")),
        name="sb_prompt",
    )(bias.reshape(1, H_SB), qkv3, qkv3, qkv3)
    return out.reshape(m, H_SB * HD)


def _sb_sample_kernel(pt_ref, bias_ref, q_ref, *rest, n_new, pages_per_step):
    blocks_per_page = 2 * H_SB // SB_BLOCK_HEADS
    new_refs = rest[:blocks_per_page]
    page_refs = rest[blocks_per_page:blocks_per_page * (1 + pages_per_step)]
    o_ref, acc_ref, surv_ref = rest[blocks_per_page * (1 + pages_per_step):]
    p = pl.program_id(1)
    row = lax.broadcasted_iota(jnp.int32, (PAGE_SIZE, PAGE_SIZE), 0)
    col = lax.broadcasted_iota(jnp.int32, (PAGE_SIZE, PAGE_SIZE), 1)
    later = (row > col).astype(BF16)
    later2 = jnp.concatenate([later, later], axis=0)
    qs = [(q_ref[0, h] * SB_SCALE).astype(BF16) for h in range(H_SB)]

    def head_tile(ref, i):
        flat = ref.reshape(PAGE_SIZE * SB_BLOCK_HEADS, HD)
        return flat[pl.ds(i, PAGE_SIZE, stride=SB_BLOCK_HEADS), :].astype(BF16)

    def page(blocks, mask, surv):
        zs = []
        for h in range(H_SB):
            kh = head_tile(blocks[h // SB_BLOCK_HEADS], h % SB_BLOCK_HEADS)
            z = lax.dot_general(qs[h], kh, (((1,), (1,)), ((), ())), preferred_element_type=F32)
            zs.append(z + bias_ref[0, h])
        z = jnp.concatenate(zs, axis=0)
        t = jnp.log(1.0 + jnp.exp(-jnp.abs(z)))
        ls = jnp.minimum(z, 0.0) - t
        lneg = ls - z
        if mask is not None:
            lneg = jnp.where(mask, lneg, 0.0)
        hi, lo = _split_bf16(lneg)
        between = jnp.dot(jnp.concatenate([hi, lo], axis=1), later2, preferred_element_type=F32)
        a = jnp.exp(ls + between + surv)
        if mask is not None:
            a = jnp.where(mask, a, 0.0)
        outs = []
        v0 = H_SB // SB_BLOCK_HEADS
        for h in range(H_SB):
            vh = head_tile(blocks[v0 + h // SB_BLOCK_HEADS], h % SB_BLOCK_HEADS)
            ah = a[h * SB_Q_SLOTS:(h + 1) * SB_Q_SLOTS].astype(BF16)
            outs.append(jnp.dot(ah, vh, preferred_element_type=F32))
        return jnp.concatenate(outs, axis=0), surv + jnp.sum(lneg, axis=-1, keepdims=True)

    @pl.when(p == 0)
    def _():
        mask = (col < (row & (SB_Q_SLOTS - 1))) & (col < n_new)
        contrib, surv = page(new_refs, mask, jnp.zeros((H_SB * SB_Q_SLOTS, 1), F32))
        acc_ref[...] = contrib
        surv_ref[...] = surv

    surv = surv_ref[...]
    total = acc_ref[...]
    for j in range(pages_per_step):
        contrib, surv = page(page_refs[j * blocks_per_page:(j + 1) * blocks_per_page], None, surv)
        total = total + contrib
    acc_ref[...] = total
    surv_ref[...] = surv

    @pl.when(p == pl.num_programs(1) - 1)
    def _():
        for h in range(H_SB):
            o_ref[0, :, h * HD:(h + 1) * HD] = acc_ref[h * SB_Q_SLOTS:(h + 1) * SB_Q_SLOTS, :].astype(o_ref.dtype)


def _sb_sample(qkv, pools, layer, page_table, bias, *, n_batch, n_new, pages_per_step=4):
    hw = H_SB * HD
    n_pool = pools.shape[1]
    n_pages = page_table.shape[1]
    blocks_per_page = 2 * H_SB // SB_BLOCK_HEADS
    q = qkv[:, :hw].reshape(n_batch, n_new, H_SB, HD).transpose(0, 2, 1, 3)
    q = jnp.pad(q, ((0, 0), (0, 0), (0, SB_Q_SLOTS - n_new), (0, 0)))
    new_kv = jnp.pad(qkv[:, hw:].reshape(n_batch, n_new, 2 * H_SB, HD),
                     ((0, 0), (0, PAGE_SIZE - n_new), (0, 0), (0, 0)))
    pool4 = pools.reshape(pools.shape[0] * n_pool, PAGE_SIZE, 2 * H_SB, HD)
    blk = (1, PAGE_SIZE, SB_BLOCK_HEADS, HD)

    def page_spec(j, hb):
        return pl.BlockSpec(
            blk, lambda b, p, pt: (layer * n_pool + pt[b, n_pages - 1 - (p * pages_per_step + j)], 0, hb, 0))

    grid_spec = pltpu.PrefetchScalarGridSpec(
        num_scalar_prefetch=1,
        grid=(n_batch, n_pages // pages_per_step),
        in_specs=[pl.BlockSpec(memory_space=pltpu.SMEM),
                  pl.BlockSpec((1, H_SB, SB_Q_SLOTS, HD), lambda b, p, pt: (b, 0, 0, 0))]
        + [pl.BlockSpec(blk, lambda b, p, pt, hb=hb: (b, 0, hb, 0)) for hb in range(blocks_per_page)]
        + [page_spec(j, hb) for j in range(pages_per_step) for hb in range(blocks_per_page)],
        out_specs=pl.BlockSpec((1, SB_Q_SLOTS, hw), lambda b, p, pt: (b, 0, 0)),
        scratch_shapes=[pltpu.VMEM((H_SB * SB_Q_SLOTS, HD), F32), pltpu.VMEM((H_SB * SB_Q_SLOTS, 1), F32)],
    )
    out = pl.pallas_call(
        functools.partial(_sb_sample_kernel, n_new=n_new, pages_per_step=pages_per_step),
        grid_spec=grid_spec,
        out_shape=jax.ShapeDtypeStruct((n_batch, SB_Q_SLOTS, hw), BF16),
        compiler_params=_cparams(("parallel", "arbitrary")),
        name="sb_sample",
    )(page_table, bias.reshape(1, H_SB), q, *([new_kv] * blocks_per_page),
      *([pool4] * (pages_per_step * blocks_per_page)))
    return out[:, :n_new].reshape(n_batch * n_new, hw)


def _tiles(m):
    return 1024 if m % 1024 == 0 else m


def _ffn(x, gain, w_gate, w_up, w_down):
    m = x.shape[0]
    d_ff = w_gate.shape[1]
    zeros = jnp.zeros((1, d_ff), F32)
    a = _norm_dual(x, gain, w_gate, w_up, zeros, zeros, act="swiglu", col0=0, col1=0, n_out=d_ff,
                   out_dtype=BF16, tm=_tiles(m), tn=512)
    return _res_matmul(a, w_down, jnp.zeros((x.shape[1],), F32), x, tm=_tiles(m), tn=256)


def _conv_layer(x, gain, hist, w1, b1, w_dw, b_dw, ln_g, ln_b, w2, b2, *, n_batch):
    m, d = x.shape
    t = m // n_batch
    b1r = b1.reshape(1, 2 * d)
    g = _norm_dual(x, gain, w1, w1, b1r, b1r, act="glu", col0=0, col1=d, n_out=d, out_dtype=F32,
                   tm=_tiles(m), tn=512)
    if t >= 256:
        tt, g_pad = 256, g
    else:
        tt = 8
        g_pad = jnp.pad(g.reshape(n_batch, t, d), ((0, 0), (0, tt - t), (0, 0))).reshape(n_batch * tt, d)
    hist_pad = jnp.pad(hist, ((0, 0), (CONV_HALO - CONV_HIST, 0), (0, 0)))
    s = _conv_core(g_pad, hist_pad, w_dw, b_dw, ln_g, ln_b, n_batch=n_batch, tt=tt)
    if tt != 256:
        s = s.reshape(n_batch, tt, d)[:, :t].reshape(m, d)
    x = _res_matmul(s, w2, b2, x, tm=_tiles(m), tn=512)
    state = jnp.concatenate([hist, g.reshape(n_batch, t, d)], axis=1)[:, -CONV_HIST:]
    return x, state


def _dil_qkv(x, gain, w_qkv, q_gain, k_gain):
    m = x.shape[0]
    third = w_qkv.shape[1] // 3
    return _norm_single(x, gain, w_qkv, jnp.stack([q_gain, k_gain]), n_normed_cols=2 * third, tm=_tiles(m), tn=512)


def _dil_states(qkv, n_batch):
    t = qkv.shape[0] // n_batch
    q5 = qkv.reshape(n_batch, t, 3, len(DIL_GROUPS), H_DIL, HD)
    return [jnp.stack([q5[:, :, 1, g], q5[:, :, 2, g]], axis=2) for g in range(len(DIL_GROUPS))]


def kernel(x_prompt, x_sample, state_conv, cache_win_g0, cache_win_g1, cache_win_g2, cache_sb_kv, page_table, ln_mix, ln_ffn, conv_w1, conv_b1, conv_w_dw, conv_b_dw, conv_ln_g, conv_ln_b, conv_w2, conv_b2, dil_w_qkv, dil_q_gain, dil_k_gain, dil_w_o, sb_w_qkv, sb_w_o, sb_bias, ffn_w_gate, ffn_w_up, ffn_w_down):
    bp, tp, d = x_prompt.shape
    bs, ts, _ = x_sample.shape
    depth = ln_mix.shape[0]
    xp = x_prompt.reshape(bp * tp, d)
    xs = x_sample.reshape(bs * ts, d)
    win_caches = (cache_win_g0, cache_win_g1, cache_win_g2)
    conv_p, conv_s, sb_p, sb_s = [], [], [], []
    win_p = [[] for _ in DIL_GROUPS]
    win_s = [[] for _ in DIL_GROUPS]
    i_conv = i_dil = i_sb = 0
    zero_d = jnp.zeros((d,), F32)
    for i in range(depth):
        kind = i % 3
        if kind == 0:
            j = i_conv
            prm = (conv_w1[j], conv_b1[j], conv_w_dw[j], conv_b_dw[j], conv_ln_g[j], conv_ln_b[j], conv_w2[j], conv_b2[j])
            xp, st_p = _conv_layer(xp, ln_mix[i], jnp.zeros((bp, CONV_HIST, d), F32), *prm, n_batch=bp)
            xs, st_s = _conv_layer(xs, ln_mix[i], state_conv[j], *prm, n_batch=bs)
            conv_p.append(st_p)
            conv_s.append(st_s)
            i_conv += 1
        elif kind == 1:
            j = i_dil
            qkv_p = _dil_qkv(xp, ln_mix[i], dil_w_qkv[j], dil_q_gain[j], dil_k_gain[j])
            qkv_s = _dil_qkv(xs, ln_mix[i], dil_w_qkv[j], dil_q_gain[j], dil_k_gain[j])
            o_p = _dil_prompt(qkv_p, n_batch=bp)
            o_s = _dil_sample(qkv_s, [c[j] for c in win_caches], n_batch=bs, n_new=ts)
            xp = _res_matmul(o_p, dil_w_o[j], zero_d, xp, tm=_tiles(xp.shape[0]), tn=512)
            xs = _res_matmul(o_s, dil_w_o[j], zero_d, xs, tm=_tiles(xs.shape[0]), tn=512)
            new_p = _dil_states(qkv_p, bp)
            new_s = _dil_states(qkv_s, bs)
            for g, (window, _) in enumerate(DIL_GROUPS):
                win_p[g].append(new_p[g][:, -min(window, tp):])
                length = win_caches[g].shape[2]
                win_s[g].append(jnp.concatenate([win_caches[g][j], new_s[g]], axis=1)[:, -length:])
            i_dil += 1
        else:
            j = i_sb
            hw = H_SB * HD
            ones_gain = jnp.ones((1, HD), F32)
            qkv_p = _norm_single(xp, ln_mix[i], sb_w_qkv[j], ones_gain, n_normed_cols=0, tm=_tiles(xp.shape[0]), tn=512)
            qkv_s = _norm_single(xs, ln_mix[i], sb_w_qkv[j], ones_gain, n_normed_cols=0, tm=_tiles(xs.shape[0]), tn=512)
            o_p = _sb_prompt(qkv_p, sb_bias[j], n_batch=bp, qb=256)
            o_s = _sb_sample(qkv_s, cache_sb_kv, j, page_table, sb_bias[j], n_batch=bs, n_new=ts)
            xp = _res_matmul(o_p, sb_w_o[j], zero_d, xp, tm=_tiles(xp.shape[0]), tn=512)
            xs = _res_matmul(o_s, sb_w_o[j], zero_d, xs, tm=_tiles(xs.shape[0]), tn=512)
            sb_p.append(qkv_p[:, hw:].reshape(bp, tp, 2, H_SB, HD))
            sb_s.append(qkv_s[:, hw:].reshape(bs, ts, 2, H_SB, HD))
            i_sb += 1
        xp = _ffn(xp, ln_ffn[i], ffn_w_gate[i], ffn_w_up[i], ffn_w_down[i])
        xs = _ffn(xs, ln_ffn[i], ffn_w_gate[i], ffn_w_up[i], ffn_w_down[i])
    return (xp.reshape(bp, tp, d), xs.reshape(bs, ts, d), jnp.stack(conv_p), jnp.stack(conv_s),
            jnp.stack(win_p[0]), jnp.stack(win_s[0]), jnp.stack(win_p[1]), jnp.stack(win_s[1]),
            jnp.stack(win_p[2]), jnp.stack(win_s[2]), jnp.stack(sb_p), jnp.stack(sb_s))
```

```python
import functools

import jax
import jax.numpy as jnp
from jax import lax
from jax.experimental import pallas as pl
from jax.experimental.pallas import tpu as pltpu

F32 = jnp.float32
BF16 = jnp.bfloat16

D_MODEL = 2048
CONV_WIDTH = 31
CONV_HIST = CONV_WIDTH - 1
CONV_HALO = 32
SUBLANES = 8
SB_BLOCK_HEADS = 8
SB_Q_SLOTS = 8
DIL_GROUPS = ((128, 1), (512, 4), (2048, 16))
H_DIL = 8
HD = 128
H_SB = 16
PAGE_SIZE = 128
EPS = 1e-6
NEG_INF = -1e30
DIL_SCALE = HD ** -0.5
SB_SCALE = HD ** -0.5

VMEM_LIMIT_BYTES = 56 * 1024 * 1024
RES_TM = 2048
RES_A_DOUBLE_BUFFER_BYTES = 24 * 1024 * 1024


def _cparams(sem):
    return pltpu.CompilerParams(dimension_semantics=sem, vmem_limit_bytes=VMEM_LIMIT_BYTES)


def _log_sigmoid(z):
    return jnp.minimum(z, 0.0) - jnp.log1p(jnp.exp(-jnp.abs(z)))


def _rmsnorm_to(xn_ref, x_ref, g_ref):
    x = x_ref[...]
    ms = jnp.mean(x * x, axis=-1, keepdims=True)
    xn_ref[...] = (x * lax.rsqrt(ms + EPS) * g_ref[...]).astype(BF16)


def _norm_dual_kernel(x_ref, g_ref, w0_ref, w1_ref, b0_ref, b1_ref, o_ref, xn_ref, *, act):
    @pl.when(pl.program_id(1) == 0)
    def _():
        _rmsnorm_to(xn_ref, x_ref, g_ref)

    a = xn_ref[...]
    u0 = jnp.dot(a, w0_ref[...].astype(BF16), preferred_element_type=F32) + b0_ref[...]
    u1 = jnp.dot(a, w1_ref[...].astype(BF16), preferred_element_type=F32) + b1_ref[...]
    if act == "swiglu":
        o_ref[...] = (u0 * jax.nn.sigmoid(u0) * u1).astype(o_ref.dtype)
    else:
        o_ref[...] = (u0 * jax.nn.sigmoid(u1)).astype(o_ref.dtype)


def _norm_dual(x, gain, w0, w1, b0, b1, *, act, col0, col1, n_out, out_dtype, tm, tn):
    m, k = x.shape
    o0, o1 = col0 // tn, col1 // tn
    return pl.pallas_call(
        functools.partial(_norm_dual_kernel, act=act),
        grid=(m // tm, n_out // tn),
        in_specs=[
            pl.BlockSpec((tm, k), lambda i, j: (i, 0)),
            pl.BlockSpec((1, k), lambda i, j: (0, 0)),
            pl.BlockSpec((k, tn), lambda i, j: (0, j + o0)),
            pl.BlockSpec((k, tn), lambda i, j: (0, j + o1)),
            pl.BlockSpec((1, tn), lambda i, j: (0, j + o0)),
            pl.BlockSpec((1, tn), lambda i, j: (0, j + o1)),
        ],
        out_specs=pl.BlockSpec((tm, tn), lambda i, j: (i, j)),
        out_shape=jax.ShapeDtypeStruct((m, n_out), out_dtype),
        scratch_shapes=[pltpu.VMEM((tm, k), BF16)],
        compiler_params=_cparams(("parallel", "arbitrary")),
        name=f"norm_dual_{act}",
    )(x, gain.reshape(1, k), w0, w1, b0, b1)


def _norm_single_kernel(x_ref, g_ref, w_ref, hg_ref, o_ref, xn_ref, *, n_normed_tiles, row_chunk):
    j = pl.program_id(1)

    @pl.when(j == 0)
    def _():
        _rmsnorm_to(xn_ref, x_ref, g_ref)

    wb = w_ref[...].astype(BF16)
    tm, tn = o_ref.shape
    for r0 in range(0, tm, row_chunk):
        acc = jnp.dot(xn_ref[r0:r0 + row_chunk, :], wb, preferred_element_type=F32)
        if n_normed_tiles:
            hg = hg_ref[0]
            normed = []
            for c in range(tn // HD):
                blk = acc[:, c * HD:(c + 1) * HD]
                ms = jnp.mean(blk * blk, axis=-1, keepdims=True)
                normed.append(blk * lax.rsqrt(ms + EPS) * hg)
            acc = jnp.where(j < n_normed_tiles, jnp.concatenate(normed, axis=1), acc)
        o_ref[r0:r0 + row_chunk, :] = acc


def _norm_single(x, gain, w, head_gains, *, n_normed_cols, tm, tn):
    m, k = x.shape
    n = w.shape[1]
    n_normed_tiles = n_normed_cols // tn
    n_sec = head_gains.shape[0]
    tiles_per_sec = max(n_normed_tiles // n_sec, 1)
    return pl.pallas_call(
        functools.partial(_norm_single_kernel, n_normed_tiles=n_normed_tiles, row_chunk=min(tm, 256)),
        grid=(m // tm, n // tn),
        in_specs=[
            pl.BlockSpec((tm, k), lambda i, j: (i, 0)),
            pl.BlockSpec((1, k), lambda i, j: (0, 0)),
            pl.BlockSpec((k, tn), lambda i, j: (0, j)),
            pl.BlockSpec((1, 1, HD), lambda i, j: (jnp.minimum(j // tiles_per_sec, n_sec - 1), 0, 0)),
        ],
        out_specs=pl.BlockSpec((tm, tn), lambda i, j: (i, j)),
        out_shape=jax.ShapeDtypeStruct((m, n), F32),
        scratch_shapes=[pltpu.VMEM((tm, k), BF16)],
        compiler_params=_cparams(("parallel", "arbitrary")),
        name="norm_single",
    )(x, gain.reshape(1, k), w, head_gains.reshape(n_sec, 1, HD))


def _res_kernel(a_ref, w_ref, b_ref, r_ref, o_ref):
    acc = jnp.dot(a_ref[...], w_ref[...].astype(BF16), preferred_element_type=F32)
    o_ref[...] = r_ref[...] + (acc + b_ref[...])


def _res_matmul(a, w, bias, res):
    m, k = a.shape
    n = w.shape[1]
    tm = RES_TM if m % RES_TM == 0 else m
    tn = 512 if k <= 2048 else 256
    a_mode = dict(pipeline_mode=pl.Buffered(1)) if 2 * tm * k * a.dtype.itemsize > RES_A_DOUBLE_BUFFER_BYTES else {}
    return pl.pallas_call(
        _res_kernel,
        grid=(m // tm, n // tn),
        in_specs=[
            pl.BlockSpec((tm, k), lambda i, j: (i, 0), **a_mode),
            pl.BlockSpec((k, tn), lambda i, j: (0, j)),
            pl.BlockSpec((1, tn), lambda i, j: (0, j)),
            pl.BlockSpec((tm, tn), lambda i, j: (i, j)),
        ],
        out_specs=pl.BlockSpec((tm, tn), lambda i, j: (i, j)),
        out_shape=jax.ShapeDtypeStruct((m, n), F32),
        compiler_params=_cparams(("parallel", "arbitrary")),
        name="res_matmul",
    )(a, w, bias.reshape(1, n), res)


def _conv_kernel(hist_ref, prev_ref, cur_ref, w_ref, bdw_ref, lg_ref, lb_ref, o_ref, win_ref, sh_ref, c_ref,
                 *, tt, rows, cols):
    d = cur_ref.shape[1]
    first = pl.program_id(1) == 0
    win_ref[0:CONV_HALO, :] = jnp.where(first, hist_ref[0], prev_ref[...])
    win_ref[CONV_HALO:CONV_HALO + tt, :] = cur_ref[...]
    off = CONV_HALO - CONV_HIST
    n_sh = CONV_HALO + tt - SUBLANES
    for c0 in range(0, d, cols):
        for s in range(1, SUBLANES):
            sh_ref[s, 0:n_sh, :] = win_ref[s:s + n_sh, c0:c0 + cols]
        for r0 in range(0, tt, rows):
            acc = jnp.zeros((rows, cols), F32)
            for k in range(CONV_WIDTH):
                s = (off + k) % SUBLANES
                a = r0 + off + k - s
                tap = win_ref[a:a + rows, c0:c0 + cols] if s == 0 else sh_ref[s, a:a + rows, :]
                acc = acc + tap * w_ref[k:k + 1, c0:c0 + cols]
            c_ref[r0:r0 + rows, c0:c0 + cols] = acc + bdw_ref[:, c0:c0 + cols]
    c = c_ref[...]
    xc = c - jnp.mean(c, axis=-1, keepdims=True)
    y = xc * lax.rsqrt(jnp.mean(xc * xc, axis=-1, keepdims=True) + EPS) * lg_ref[...] + lb_ref[...]
    o_ref[...] = (y * jax.nn.sigmoid(y)).astype(o_ref.dtype)


def _conv_core(g, hist, w_dw, b_dw, ln_g, ln_b, *, n_batch, tt):
    m, d = g.shape
    t = m // n_batch
    nt = t // tt
    ratio = tt // CONV_HALO if tt >= CONV_HALO else None
    if ratio is None:
        prev_spec = pl.BlockSpec((CONV_HALO, d), lambda b, i: (0, 0))
        prev = hist.reshape(n_batch * CONV_HALO, d)
    else:
        prev_spec = pl.BlockSpec((CONV_HALO, d), lambda b, i: (jnp.maximum((b * nt + i) * ratio - 1, 0), 0))
        prev = g
    w_pad = jnp.pad(w_dw, ((0, 32 - CONV_WIDTH), (0, 0)))
    return pl.pallas_call(
        functools.partial(_conv_kernel, tt=tt, rows=min(tt, 32), cols=min(d, 512)),
        grid=(n_batch, nt),
        in_specs=[
            pl.BlockSpec((1, CONV_HALO, d), lambda b, i: (b, 0, 0)),
            prev_spec,
            pl.BlockSpec((tt, d), lambda b, i: (b * nt + i, 0)),
            pl.BlockSpec((32, d), lambda b, i: (0, 0)),
            pl.BlockSpec((1, d), lambda b, i: (0, 0)),
            pl.BlockSpec((1, d), lambda b, i: (0, 0)),
            pl.BlockSpec((1, d), lambda b, i: (0, 0)),
        ],
        out_specs=pl.BlockSpec((tt, d), lambda b, i: (b * nt + i, 0)),
        out_shape=jax.ShapeDtypeStruct((m, d), BF16),
        scratch_shapes=[pltpu.VMEM((CONV_HALO + tt, d), F32),
                        pltpu.VMEM((SUBLANES, CONV_HALO + tt - SUBLANES, min(d, 512)), F32),
                        pltpu.VMEM((tt, d), F32)],
        compiler_params=_cparams(("parallel", "arbitrary")),
        name="conv_core",
    )(hist, prev, g, w_pad, b_dw.reshape(1, d), ln_g.reshape(1, d), ln_b.reshape(1, d))


def _softmax_block(s, mask):
    s = jnp.where(mask, s, NEG_INF)
    m = jnp.max(s, axis=-1, keepdims=True)
    p = jnp.exp(s - m)
    den = jnp.sum(p, axis=-1, keepdims=True)
    return p, den, m + jnp.log(den)


def _mix_groups(outs, lses):
    m = jnp.maximum(jnp.maximum(lses[0], lses[1]), lses[2])
    es = [jnp.exp(l - m) for l in lses]
    tot = es[0] + es[1] + es[2]
    return (es[0] * outs[0] + es[1] * outs[1] + es[2] * outs[2]) / tot


DIL_UNROLL = 8


def _largest_divisor(n, cap):
    return max(u for u in range(1, cap + 1) if n % u == 0)


def _dil_prompt_kernel(q0, q1, q2, k0, k1, k2, v0, v1, v2, o_ref, og_ref, lse_ref, *, t, qb):
    q_refs, k_refs, v_refs = (q0, q1, q2), (k0, k1, k2), (v0, v1, v2)
    row = lax.broadcasted_iota(jnp.int32, (qb, 2 * qb), 0)
    col = lax.broadcasted_iota(jnp.int32, (qb, 2 * qb), 1)
    band = (col >= row) & (col <= row + qb)
    causal = (lax.broadcasted_iota(jnp.int32, (qb, qb), 1)
              <= lax.broadcasted_iota(jnp.int32, (qb, qb), 0))

    def rows(start, size, d):
        return pl.ds(start, size) if d == 1 else pl.ds(start, size, stride=d)

    for g, (window, d) in enumerate(DIL_GROUPS):
        assert window == qb * d, "a query block plus the previous one must cover the window"
        q_ref, k_ref, v_ref = q_refs[g], k_refs[g], v_refs[g]
        nblk = t // (d * qb)

        def attend(q_start, k_start, nk, mask, g=g, d=d, q_ref=q_ref, k_ref=k_ref, v_ref=v_ref):
            qv = q_ref[0, rows(q_start, qb, d), :].astype(BF16)
            kv = k_ref[0, rows(k_start, nk, d), :].astype(BF16)
            vv = v_ref[0, rows(k_start, nk, d), :].astype(BF16)
            s = lax.dot_general(qv, kv, (((1,), (1,)), ((), ())), preferred_element_type=F32) * DIL_SCALE
            p, den, lse = _softmax_block(s, mask)
            o = jnp.dot(p.astype(BF16), vv, preferred_element_type=F32) / den
            og_ref[g, rows(q_start, qb, d), :] = o
            lse_ref[g, rows(q_start, qb, d), :] = jnp.broadcast_to(lse, (qb, HD))

        def residue(r, carry, d=d, nblk=nblk, attend=attend):
            attend(r, r, qb, causal)

            def later(i, c):
                attend(r + i * qb * d, r + (i - 1) * qb * d, 2 * qb, band)
                return c

            if nblk > 1:
                lax.fori_loop(1, nblk, later, 0, unroll=_largest_divisor(nblk - 1, DIL_UNROLL))
            return carry

        lax.fori_loop(0, d, residue, 0, unroll=_largest_divisor(d, DIL_UNROLL) if nblk == 1 else 1)

    outs = [og_ref[g] for g in range(3)]
    lses = [lse_ref[g] for g in range(3)]
    o_ref[0] = _mix_groups(outs, lses).astype(o_ref.dtype)


def _dil_prompt(qkv, *, n_batch):
    m, n = qkv.shape
    t = m // n_batch
    qkv3 = qkv.reshape(n_batch, t, n)
    nh = len(DIL_GROUPS) * H_DIL

    def spec(sec, g):
        return pl.BlockSpec((1, t, HD), lambda b, h: (b, 0, sec * nh + g * H_DIL + h))

    out = pl.pallas_call(
        functools.partial(_dil_prompt_kernel, t=t, qb=128),
        grid=(n_batch, H_DIL),
        in_specs=[spec(sec, g) for sec in range(3) for g in range(3)],
        out_specs=pl.BlockSpec((1, t, HD), lambda b, h: (b, 0, h)),
        out_shape=jax.ShapeDtypeStruct((n_batch, t, H_DIL * HD), BF16),
        scratch_shapes=[pltpu.VMEM((3, t, HD), F32), pltpu.VMEM((3, t, HD), F32)],
        compiler_params=_cparams(("parallel", "parallel")),
        name="dil_prompt",
    )(*([qkv3] * 9))
    return out.reshape(m, H_DIL * HD)


def _dil_sample_kernel(qkv_ref, ck0, ck1, ck2, cv0, cv1, cv2, o_ref, *, n_new):
    ck_refs, cv_refs = (ck0, ck1, ck2), (cv0, cv1, cv2)
    nq = qkv_ref.shape[1]
    nh = len(DIL_GROUPS) * H_DIL
    n_col = lax.broadcasted_iota(jnp.int32, (nq, 1), 0)
    outs = [[None] * len(DIL_GROUPS) for _ in range(H_DIL)]
    lses = [[None] * len(DIL_GROUPS) for _ in range(H_DIL)]
    for g, (window, d) in enumerate(DIL_GROUPS):
        length = ck_refs[g].shape[1]
        assert length == window and length % d == 0
        k_flat = ck_refs[g].reshape(length * H_DIL, HD)
        v_flat = cv_refs[g].reshape(length * H_DIL, HD)
        n_i = lax.broadcasted_iota(jnp.int32, (nq, length), 0)
        c_i = lax.broadcasted_iota(jnp.int32, (nq, length), 1)
        mask = (c_i >= n_i) & (((c_i - n_i) & (d - 1)) == 0)
        for h in range(H_DIL):
            def cols(sec, g=g, h=h):
                c0 = (sec * nh + g * H_DIL + h) * HD
                return slice(c0, c0 + HD)

            qf = qkv_ref[0, :, cols(0)]
            kc = k_flat[pl.ds(h, length, stride=H_DIL), :].astype(BF16)
            vc = v_flat[pl.ds(h, length, stride=H_DIL), :].astype(BF16)
            s = lax.dot_general(qf.astype(BF16), kc, (((1,), (1,)), ((), ())), preferred_element_type=F32) * DIL_SCALE
            s = jnp.where(mask, s, NEG_INF)
            m = jnp.max(s, axis=-1, keepdims=True)
            new_scores = []
            for j in range(n_new):
                sj = jnp.sum(qf * qkv_ref[0, j:j + 1, cols(1)], axis=-1, keepdims=True) * DIL_SCALE
                sj = jnp.where((n_col >= j) & (((n_col - j) & (d - 1)) == 0), sj, NEG_INF)
                new_scores.append(sj)
                m = jnp.maximum(m, sj)
            p = jnp.exp(s - m)
            den = jnp.sum(p, axis=-1, keepdims=True)
            o = jnp.dot(p.astype(BF16), vc, preferred_element_type=F32)
            for j in range(n_new):
                pj = jnp.exp(new_scores[j] - m)
                den = den + pj
                o = o + pj * qkv_ref[0, j:j + 1, cols(2)]
            outs[h][g] = o / den
            lses[h][g] = m + jnp.log(den)
    for h in range(H_DIL):
        o_ref[0, :, h * HD:(h + 1) * HD] = _mix_groups(outs[h], lses[h]).astype(o_ref.dtype)


def _dil_sample(qkv, caches, layer, *, n_batch, n_new):
    n = qkv.shape[1]
    nq = SUBLANES
    qkv3 = jnp.pad(qkv.reshape(n_batch, n_new, n), ((0, 0), (0, nq - n_new), (0, 0)))
    flat = [c.reshape(c.shape[0] * n_batch, c.shape[2], 2 * H_DIL, HD) for c in caches]
    ck_specs = [pl.BlockSpec((1, c.shape[1], H_DIL, HD), lambda b: (layer * n_batch + b, 0, 0, 0)) for c in flat]
    cv_specs = [pl.BlockSpec((1, c.shape[1], H_DIL, HD), lambda b: (layer * n_batch + b, 0, 1, 0)) for c in flat]
    out = pl.pallas_call(
        functools.partial(_dil_sample_kernel, n_new=n_new),
        grid=(n_batch,),
        in_specs=[pl.BlockSpec((1, nq, n), lambda b: (b, 0, 0))] + ck_specs + cv_specs,
        out_specs=pl.BlockSpec((1, nq, H_DIL * HD), lambda b: (b, 0, 0)),
        out_shape=jax.ShapeDtypeStruct((n_batch, nq, H_DIL * HD), BF16),
        compiler_params=_cparams(("parallel",)),
        name="dil_sample",
    )(qkv3, *flat, *flat)
    return out[:, :n_new].reshape(n_batch * n_new, H_DIL * HD)


def _split_bf16(x):
    hi = x.astype(BF16)
    lo = (x - hi.astype(F32)).astype(BF16)
    return hi, lo


def _sb_prompt_kernel(bias_ref, q_ref, k_ref, v_ref, o_ref, acc_ref, surv_ref, *, qb, heads):
    hp = pl.program_id(1)
    i = pl.program_id(2)
    row = lax.broadcasted_iota(jnp.int32, (qb, qb), 0)
    col = lax.broadcasted_iota(jnp.int32, (qb, qb), 1)
    later = (row > col).astype(BF16)
    later2 = jnp.concatenate([later, later], axis=0)
    strict = col < row
    lanes = [slice(hh * HD, (hh + 1) * HD) for hh in range(heads)]
    biases = [bias_ref[0, hp * heads + hh] for hh in range(heads)]
    qs = [(q_ref[0, :, lanes[hh]] * SB_SCALE).astype(BF16) for hh in range(heads)]

    def blocks(k_start, mask):
        zs = []
        for hh in range(heads):
            kv = k_ref[0, pl.ds(k_start, qb), lanes[hh]].astype(BF16)
            zs.append(lax.dot_general(qs[hh], kv, (((1,), (1,)), ((), ())), preferred_element_type=F32) + biases[hh])
        lss, lnegs, betweens = [], [], []
        for hh in range(heads):
            z = zs[hh]
            t = jnp.log(1.0 + jnp.exp(-jnp.abs(z)))
            ls = jnp.minimum(z, 0.0) - t
            lneg = ls - z
            if mask is not None:
                lneg = jnp.where(mask, lneg, 0.0)
            hi, lo = _split_bf16(lneg)
            betweens.append(jnp.dot(jnp.concatenate([hi, lo], axis=1), later2, preferred_element_type=F32))
            lss.append(ls)
            lnegs.append(lneg)
        for hh in range(heads):
            surv = surv_ref[hh]
            a = jnp.exp(lss[hh] + betweens[hh] + surv)
            if mask is not None:
                a = jnp.where(mask, a, 0.0)
            vv = v_ref[0, pl.ds(k_start, qb), lanes[hh]].astype(BF16)
            acc_ref[hh] += jnp.dot(a.astype(BF16), vv, preferred_element_type=F32)
            surv_ref[hh] = surv + jnp.sum(lnegs[hh], axis=-1, keepdims=True)

    acc_ref[...] = jnp.zeros(acc_ref.shape, F32)
    surv_ref[...] = jnp.zeros(surv_ref.shape, F32)
    blocks(pl.multiple_of(i * qb, qb), strict)

    def older(step, carry):
        blocks(pl.multiple_of((i - 1 - step) * qb, qb), None)
        return carry

    lax.fori_loop(0, i, older, 0)
    for hh in range(heads):
        o_ref[0, :, lanes[hh]] = acc_ref[hh].astype(o_ref.dtype)


def _sb_prompt(qkv, bias, *, n_batch, qb, heads=8):
    m, n = qkv.shape
    t = m // n_batch
    qkv3 = qkv.reshape(n_batch, t, n)
    hg = H_SB // heads
    w = heads * HD
    out = pl.pallas_call(
        functools.partial(_sb_prompt_kernel, qb=qb, heads=heads),
        grid=(n_batch, hg, t // qb),
        in_specs=[
            pl.BlockSpec(memory_space=pltpu.SMEM),
            pl.BlockSpec((1, qb, w), lambda b, h, i: (b, i, h)),
            pl.BlockSpec((1, t, w), lambda b, h, i: (b, 0, hg + h)),
            pl.BlockSpec((1, t, w), lambda b, h, i: (b, 0, 2 * hg + h)),
        ],
        out_specs=pl.BlockSpec((1, qb, w), lambda b, h, i: (b, i, h)),
        out_shape=jax.ShapeDtypeStruct((n_batch, t, H_SB * HD), BF16),
        scratch_shapes=[pltpu.VMEM((heads, qb, HD), F32), pltpu.VMEM((heads, qb, 1), F32)],
        compiler_params=_cparams(("parallel", "parallel", "arbitrary")),
        name="sb_prompt",
    )(bias.reshape(1, H_SB), qkv3, qkv3, qkv3)
    return out.reshape(m, H_SB * HD)


def _sb_sample_kernel(pt_ref, bias_ref, q_ref, *rest, n_new, pages_per_step):
    blocks_per_page = 2 * H_SB // SB_BLOCK_HEADS
    new_refs = rest[:blocks_per_page]
    page_refs = rest[blocks_per_page:blocks_per_page * (1 + pages_per_step)]
    o_ref, acc_ref, surv_ref = rest[blocks_per_page * (1 + pages_per_step):]
    p = pl.program_id(1)
    row = lax.broadcasted_iota(jnp.int32, (PAGE_SIZE, PAGE_SIZE), 0)
    col = lax.broadcasted_iota(jnp.int32, (PAGE_SIZE, PAGE_SIZE), 1)
    later = (row > col).astype(BF16)
    later2 = jnp.concatenate([later, later], axis=0)
    qs = [(q_ref[0, h] * SB_SCALE).astype(BF16) for h in range(H_SB)]

    def head_tile(ref, i):
        flat = ref.reshape(PAGE_SIZE * SB_BLOCK_HEADS, HD)
        return flat[pl.ds(i, PAGE_SIZE, stride=SB_BLOCK_HEADS), :].astype(BF16)

    def page(blocks, mask, surv):
        zs = []
        for h in range(H_SB):
            kh = head_tile(blocks[h // SB_BLOCK_HEADS], h % SB_BLOCK_HEADS)
            z = lax.dot_general(qs[h], kh, (((1,), (1,)), ((), ())), preferred_element_type=F32)
            zs.append(z + bias_ref[0, h])
        z = jnp.concatenate(zs, axis=0)
        t = jnp.log(1.0 + jnp.exp(-jnp.abs(z)))
        ls = jnp.minimum(z, 0.0) - t
        lneg = ls - z
        if mask is not None:
            lneg = jnp.where(mask, lneg, 0.0)
        hi, lo = _split_bf16(lneg)
        between = jnp.dot(jnp.concatenate([hi, lo], axis=1), later2, preferred_element_type=F32)
        a = jnp.exp(ls + between + surv)
        if mask is not None:
            a = jnp.where(mask, a, 0.0)
        outs = []
        v0 = H_SB // SB_BLOCK_HEADS
        for h in range(H_SB):
            vh = head_tile(blocks[v0 + h // SB_BLOCK_HEADS], h % SB_BLOCK_HEADS)
            ah = a[h * SB_Q_SLOTS:(h + 1) * SB_Q_SLOTS].astype(BF16)
            outs.append(jnp.dot(ah, vh, preferred_element_type=F32))
        return jnp.concatenate(outs, axis=0), surv + jnp.sum(lneg, axis=-1, keepdims=True)

    @pl.when(p == 0)
    def _():
        mask = (col < (row & (SB_Q_SLOTS - 1))) & (col < n_new)
        contrib, surv = page(new_refs, mask, jnp.zeros((H_SB * SB_Q_SLOTS, 1), F32))
        acc_ref[...] = contrib
        surv_ref[...] = surv

    surv = surv_ref[...]
    total = acc_ref[...]
    for j in range(pages_per_step):
        contrib, surv = page(page_refs[j * blocks_per_page:(j + 1) * blocks_per_page], None, surv)
        total = total + contrib
    acc_ref[...] = total
    surv_ref[...] = surv

    @pl.when(p == pl.num_programs(1) - 1)
    def _():
        for h in range(H_SB):
            o_ref[0, :, h * HD:(h + 1) * HD] = acc_ref[h * SB_Q_SLOTS:(h + 1) * SB_Q_SLOTS, :].astype(o_ref.dtype)


def _sb_sample(qkv, pools, layer, page_table, bias, *, n_batch, n_new, pages_per_step=8):
    hw = H_SB * HD
    n_pool = pools.shape[1]
    n_pages = page_table.shape[1]
    blocks_per_page = 2 * H_SB // SB_BLOCK_HEADS
    q = qkv[:, :hw].reshape(n_batch, n_new, H_SB, HD).transpose(0, 2, 1, 3)
    q = jnp.pad(q, ((0, 0), (0, 0), (0, SB_Q_SLOTS - n_new), (0, 0)))
    new_kv = jnp.pad(qkv[:, hw:].reshape(n_batch, n_new, 2 * H_SB, HD),
                     ((0, 0), (0, PAGE_SIZE - n_new), (0, 0), (0, 0)))
    pool4 = pools.reshape(pools.shape[0] * n_pool, PAGE_SIZE, 2 * H_SB, HD)
    blk = (1, PAGE_SIZE, SB_BLOCK_HEADS, HD)

    def page_spec(j, hb):
        return pl.BlockSpec(
            blk, lambda b, p, pt: (layer * n_pool + pt[b, n_pages - 1 - (p * pages_per_step + j)], 0, hb, 0))

    grid_spec = pltpu.PrefetchScalarGridSpec(
        num_scalar_prefetch=1,
        grid=(n_batch, n_pages // pages_per_step),
        in_specs=[pl.BlockSpec(memory_space=pltpu.SMEM),
                  pl.BlockSpec((1, H_SB, SB_Q_SLOTS, HD), lambda b, p, pt: (b, 0, 0, 0))]
        + [pl.BlockSpec(blk, lambda b, p, pt, hb=hb: (b, 0, hb, 0)) for hb in range(blocks_per_page)]
        + [page_spec(j, hb) for j in range(pages_per_step) for hb in range(blocks_per_page)],
        out_specs=pl.BlockSpec((1, SB_Q_SLOTS, hw), lambda b, p, pt: (b, 0, 0)),
        scratch_shapes=[pltpu.VMEM((H_SB * SB_Q_SLOTS, HD), F32), pltpu.VMEM((H_SB * SB_Q_SLOTS, 1), F32)],
    )
    out = pl.pallas_call(
        functools.partial(_sb_sample_kernel, n_new=n_new, pages_per_step=pages_per_step),
        grid_spec=grid_spec,
        out_shape=jax.ShapeDtypeStruct((n_batch, SB_Q_SLOTS, hw), BF16),
        compiler_params=_cparams(("parallel", "arbitrary")),
        name="sb_sample",
    )(page_table, bias.reshape(1, H_SB), q, *([new_kv] * blocks_per_page),
      *([pool4] * (pages_per_step * blocks_per_page)))
    return out[:, :n_new].reshape(n_batch * n_new, hw)


def _heads_to_rows_kernel(x_ref, o_ref):
    tm = x_ref.shape[0]
    flat = o_ref.reshape(tm * SUBLANES, HD)
    for c in range(SUBLANES):
        flat[pl.ds(c, tm, stride=SUBLANES), :] = x_ref[:, c * HD:(c + 1) * HD]


def _heads_to_rows(x, *, n_batch, length, first_block, block_stride, n_blocks):
    m, _ = x.shape
    t = m // n_batch
    tm = min(length, 512)
    nt = length // tm
    first = (t - length) // tm
    w = SUBLANES * HD
    return pl.pallas_call(
        _heads_to_rows_kernel,
        grid=(n_batch, nt, n_blocks),
        in_specs=[pl.BlockSpec((tm, w), lambda b, i, j: (b * (t // tm) + first + i, first_block + j * block_stride))],
        out_specs=pl.BlockSpec((tm, SUBLANES, HD), lambda b, i, j: (b * nt + i, j, 0)),
        out_shape=jax.ShapeDtypeStruct((n_batch * length, n_blocks * SUBLANES, HD), F32),
        compiler_params=_cparams(("parallel", "parallel", "parallel")),
        name="heads_to_rows",
    )(x)


def _tiles(m):
    return 1024 if m % 1024 == 0 else m


def _ffn(x, gain, w_gate, w_up, w_down):
    m = x.shape[0]
    d_ff = w_gate.shape[1]
    zeros = jnp.zeros((1, d_ff), F32)
    a = _norm_dual(x, gain, w_gate, w_up, zeros, zeros, act="swiglu", col0=0, col1=0, n_out=d_ff,
                   out_dtype=BF16, tm=_tiles(m), tn=512)
    return _res_matmul(a, w_down, jnp.zeros((x.shape[1],), F32), x)


def _conv_layer(x, gain, hist, w1, b1, w_dw, b_dw, ln_g, ln_b, w2, b2, *, n_batch):
    m, d = x.shape
    t = m // n_batch
    b1r = b1.reshape(1, 2 * d)
    g = _norm_dual(x, gain, w1, w1, b1r, b1r, act="glu", col0=0, col1=d, n_out=d, out_dtype=F32,
                   tm=_tiles(m), tn=512)
    if t >= 256:
        tt, g_pad = 256, g
    else:
        tt = 8
        g_pad = jnp.pad(g.reshape(n_batch, t, d), ((0, 0), (0, tt - t), (0, 0))).reshape(n_batch * tt, d)
    hist_pad = jnp.pad(hist, ((0, 0), (CONV_HALO - CONV_HIST, 0), (0, 0)))
    s = _conv_core(g_pad, hist_pad, w_dw, b_dw, ln_g, ln_b, n_batch=n_batch, tt=tt)
    if tt != 256:
        s = s.reshape(n_batch, tt, d)[:, :t].reshape(m, d)
    x = _res_matmul(s, w2, b2, x)
    state = jnp.concatenate([hist, g.reshape(n_batch, t, d)], axis=1)[:, -CONV_HIST:]
    return x, state


def _dil_qkv(x, gain, w_qkv, q_gain, k_gain):
    m = x.shape[0]
    third = w_qkv.shape[1] // 3
    return _norm_single(x, gain, w_qkv, jnp.stack([q_gain, k_gain]), n_normed_cols=2 * third, tm=_tiles(m), tn=512)


def _dil_states(qkv, n_batch):
    t = qkv.shape[0] // n_batch
    q5 = qkv.reshape(n_batch, t, 3, len(DIL_GROUPS), H_DIL, HD)
    return [jnp.stack([q5[:, :, 1, g], q5[:, :, 2, g]], axis=2) for g in range(len(DIL_GROUPS))]


def kernel(x_prompt, x_sample, state_conv, cache_win_g0, cache_win_g1, cache_win_g2, cache_sb_kv, page_table, ln_mix, ln_ffn, conv_w1, conv_b1, conv_w_dw, conv_b_dw, conv_ln_g, conv_ln_b, conv_w2, conv_b2, dil_w_qkv, dil_q_gain, dil_k_gain, dil_w_o, sb_w_qkv, sb_w_o, sb_bias, ffn_w_gate, ffn_w_up, ffn_w_down):
    bp, tp, d = x_prompt.shape
    bs, ts, _ = x_sample.shape
    depth = ln_mix.shape[0]
    xp = x_prompt.reshape(bp * tp, d)
    xs = x_sample.reshape(bs * ts, d)
    win_caches = (cache_win_g0, cache_win_g1, cache_win_g2)
    conv_p, conv_s, sb_p, sb_s = [], [], [], []
    win_p = [[] for _ in DIL_GROUPS]
    win_s = [[] for _ in DIL_GROUPS]
    i_conv = i_dil = i_sb = 0
    zero_d = jnp.zeros((d,), F32)
    for i in range(depth):
        kind = i % 3
        if kind == 0:
            j = i_conv
            prm = (conv_w1[j], conv_b1[j], conv_w_dw[j], conv_b_dw[j], conv_ln_g[j], conv_ln_b[j], conv_w2[j], conv_b2[j])
            xp, st_p = _conv_layer(xp, ln_mix[i], jnp.zeros((bp, CONV_HIST, d), F32), *prm, n_batch=bp)
            xs, st_s = _conv_layer(xs, ln_mix[i], state_conv[j], *prm, n_batch=bs)
            conv_p.append(st_p)
            conv_s.append(st_s)
            i_conv += 1
        elif kind == 1:
            j = i_dil
            qkv_p = _dil_qkv(xp, ln_mix[i], dil_w_qkv[j], dil_q_gain[j], dil_k_gain[j])
            qkv_s = _dil_qkv(xs, ln_mix[i], dil_w_qkv[j], dil_q_gain[j], dil_k_gain[j])
            o_p = _dil_prompt(qkv_p, n_batch=bp)
            o_s = _dil_sample(qkv_s, win_caches, j, n_batch=bs, n_new=ts)
            xp = _res_matmul(o_p, dil_w_o[j], zero_d, xp)
            xs = _res_matmul(o_s, dil_w_o[j], zero_d, xs)
            new_s = _dil_states(qkv_s, bs)
            n_g = len(DIL_GROUPS)
            for g, (window, _) in enumerate(DIL_GROUPS):
                keep = min(window, tp)
                st = _heads_to_rows(qkv_p, n_batch=bp, length=keep, first_block=n_g + g, block_stride=n_g, n_blocks=2)
                win_p[g].append(st.reshape(bp, keep, 2, H_DIL, HD))
                length = win_caches[g].shape[2]
                win_s[g].append(jnp.concatenate([win_caches[g][j], new_s[g]], axis=1)[:, -length:])
            i_dil += 1
        else:
            j = i_sb
            hw = H_SB * HD
            ones_gain = jnp.ones((1, HD), F32)
            qkv_p = _norm_single(xp, ln_mix[i], sb_w_qkv[j], ones_gain, n_normed_cols=0, tm=_tiles(xp.shape[0]), tn=512)
            qkv_s = _norm_single(xs, ln_mix[i], sb_w_qkv[j], ones_gain, n_normed_cols=0, tm=_tiles(xs.shape[0]), tn=512)
            o_p = _sb_prompt(qkv_p, sb_bias[j], n_batch=bp, qb=256)
            o_s = _sb_sample(qkv_s, cache_sb_kv, j, page_table, sb_bias[j], n_batch=bs, n_new=ts)
            xp = _res_matmul(o_p, sb_w_o[j], zero_d, xp)
            xs = _res_matmul(o_s, sb_w_o[j], zero_d, xs)
            kv_blocks = 2 * H_SB // SUBLANES
            st = _heads_to_rows(qkv_p, n_batch=bp, length=tp, first_block=kv_blocks // 2, block_stride=1, n_blocks=kv_blocks)
            sb_p.append(st.reshape(bp, tp, 2, H_SB, HD))
            sb_s.append(qkv_s[:, hw:].reshape(bs, ts, 2, H_SB, HD))
            i_sb += 1
        xp = _ffn(xp, ln_ffn[i], ffn_w_gate[i], ffn_w_up[i], ffn_w_down[i])
        xs = _ffn(xs, ln_ffn[i], ffn_w_gate[i], ffn_w_up[i], ffn_w_down[i])
    return (xp.reshape(bp, tp, d), xs.reshape(bs, ts, d), jnp.stack(conv_p), jnp.stack(conv_s),
            jnp.stack(win_p[0]), jnp.stack(win_s[0]), jnp.stack(win_p[1]), jnp.stack(win_s[1]),
            jnp.stack(win_p[2]), jnp.stack(win_s[2]), jnp.stack(sb_p), jnp.stack(sb_s))
```

```python
import functools

import jax
import jax.numpy as jnp
from jax import lax
from jax.experimental import pallas as pl
from jax.experimental.pallas import tpu as pltpu

F32 = jnp.float32
BF16 = jnp.bfloat16

D_MODEL = 2048
CONV_WIDTH = 31
CONV_HIST = CONV_WIDTH - 1
CONV_HALO = 32
SUBLANES = 8
SB_BLOCK_HEADS = 8
SB_Q_SLOTS = 8
DIL_GROUPS = ((128, 1), (512, 4), (2048, 16))
H_DIL = 8
HD = 128
H_SB = 16
PAGE_SIZE = 128
EPS = 1e-6
NEG_INF = -1e30
DIL_SCALE = HD ** -0.5
SB_SCALE = HD ** -0.5

VMEM_LIMIT_BYTES = 56 * 1024 * 1024
RES_TM = (2048, 1024)
RES_A_BYTES = 24 * 1024 * 1024


def _cparams(sem):
    return pltpu.CompilerParams(dimension_semantics=sem, vmem_limit_bytes=VMEM_LIMIT_BYTES)


def _log_sigmoid(z):
    return jnp.minimum(z, 0.0) - jnp.log1p(jnp.exp(-jnp.abs(z)))


def _rmsnorm_to(xn_ref, x_ref, g_ref):
    x = x_ref[...]
    ms = jnp.mean(x * x, axis=-1, keepdims=True)
    xn_ref[...] = (x * lax.rsqrt(ms + EPS) * g_ref[...]).astype(BF16)


def _norm_dual_kernel(x_ref, g_ref, w0_ref, w1_ref, b0_ref, b1_ref, o_ref, xn_ref, *, act):
    @pl.when(pl.program_id(1) == 0)
    def _():
        _rmsnorm_to(xn_ref, x_ref, g_ref)

    a = xn_ref[...]
    u0 = jnp.dot(a, w0_ref[...].astype(BF16), preferred_element_type=F32) + b0_ref[...]
    u1 = jnp.dot(a, w1_ref[...].astype(BF16), preferred_element_type=F32) + b1_ref[...]
    if act == "swiglu":
        o_ref[...] = (u0 * jax.nn.sigmoid(u0) * u1).astype(o_ref.dtype)
    else:
        o_ref[...] = (u0 * jax.nn.sigmoid(u1)).astype(o_ref.dtype)


def _layer_weight_spec(k, tn, layer, col_block=0):
    return pl.BlockSpec((None, k, tn), lambda i, j: (layer, 0, j + col_block))


def _norm_dual(x, gain, w0, w1, layer, b0, b1, *, act, col0, col1, n_out, out_dtype, tm, tn):
    m, k = x.shape
    o0, o1 = col0 // tn, col1 // tn
    return pl.pallas_call(
        functools.partial(_norm_dual_kernel, act=act),
        grid=(m // tm, n_out // tn),
        in_specs=[
            pl.BlockSpec((tm, k), lambda i, j: (i, 0)),
            pl.BlockSpec((1, k), lambda i, j: (0, 0)),
            _layer_weight_spec(k, tn, layer, o0),
            _layer_weight_spec(k, tn, layer, o1),
            pl.BlockSpec((1, tn), lambda i, j: (0, j + o0)),
            pl.BlockSpec((1, tn), lambda i, j: (0, j + o1)),
        ],
        out_specs=pl.BlockSpec((tm, tn), lambda i, j: (i, j)),
        out_shape=jax.ShapeDtypeStruct((m, n_out), out_dtype),
        scratch_shapes=[pltpu.VMEM((tm, k), BF16)],
        compiler_params=_cparams(("parallel", "arbitrary")),
        name=f"norm_dual_{act}",
    )(x, gain.reshape(1, k), w0, w1, b0, b1)


def _norm_single_kernel(x_ref, g_ref, w_ref, hg_ref, o_ref, xn_ref, *, n_normed_tiles, row_chunk):
    j = pl.program_id(1)

    @pl.when(j == 0)
    def _():
        _rmsnorm_to(xn_ref, x_ref, g_ref)

    wb = w_ref[...].astype(BF16)
    tm, tn = o_ref.shape
    for r0 in range(0, tm, row_chunk):
        acc = jnp.dot(xn_ref[r0:r0 + row_chunk, :], wb, preferred_element_type=F32)
        if n_normed_tiles:
            hg = hg_ref[0]
            normed = []
            for c in range(tn // HD):
                blk = acc[:, c * HD:(c + 1) * HD]
                ms = jnp.mean(blk * blk, axis=-1, keepdims=True)
                normed.append(blk * lax.rsqrt(ms + EPS) * hg)
            acc = jnp.where(j < n_normed_tiles, jnp.concatenate(normed, axis=1), acc)
        o_ref[r0:r0 + row_chunk, :] = acc


def _norm_single(x, gain, w, layer, head_gains, *, n_normed_cols, tm, tn):
    m, k = x.shape
    n = w.shape[2]
    n_normed_tiles = n_normed_cols // tn
    n_sec = head_gains.shape[0]
    tiles_per_sec = max(n_normed_tiles // n_sec, 1)
    return pl.pallas_call(
        functools.partial(_norm_single_kernel, n_normed_tiles=n_normed_tiles, row_chunk=min(tm, 256)),
        grid=(m // tm, n // tn),
        in_specs=[
            pl.BlockSpec((tm, k), lambda i, j: (i, 0)),
            pl.BlockSpec((1, k), lambda i, j: (0, 0)),
            _layer_weight_spec(k, tn, layer),
            pl.BlockSpec((1, 1, HD), lambda i, j: (jnp.minimum(j // tiles_per_sec, n_sec - 1), 0, 0)),
        ],
        out_specs=pl.BlockSpec((tm, tn), lambda i, j: (i, j)),
        out_shape=jax.ShapeDtypeStruct((m, n), F32),
        scratch_shapes=[pltpu.VMEM((tm, k), BF16)],
        compiler_params=_cparams(("parallel", "arbitrary")),
        name="norm_single",
    )(x, gain.reshape(1, k), w, head_gains.reshape(n_sec, 1, HD))


def _res_kernel(a_ref, w_ref, b_ref, r_ref, o_ref):
    acc = jnp.dot(a_ref[...], w_ref[...].astype(BF16), preferred_element_type=F32)
    o_ref[...] = r_ref[...] + (acc + b_ref[...])


def _res_matmul(a, w, layer, bias, res):
    m, k = a.shape
    n = w.shape[2]
    tm = next((t for t in RES_TM if m % t == 0 and 2 * t * k * a.dtype.itemsize <= RES_A_BYTES), m)
    tn = 512 if k <= 2048 else 256
    return pl.pallas_call(
        _res_kernel,
        grid=(m // tm, n // tn),
        in_specs=[
            pl.BlockSpec((tm, k), lambda i, j: (i, 0)),
            _layer_weight_spec(k, tn, layer),
            pl.BlockSpec((1, tn), lambda i, j: (0, j)),
            pl.BlockSpec((tm, tn), lambda i, j: (i, j)),
        ],
        out_specs=pl.BlockSpec((tm, tn), lambda i, j: (i, j)),
        out_shape=jax.ShapeDtypeStruct((m, n), F32),
        compiler_params=_cparams(("parallel", "arbitrary")),
        name="res_matmul",
    )(a, w, bias.reshape(1, n), res)


def _conv_kernel(hist_ref, prev_ref, cur_ref, w_ref, bdw_ref, lg_ref, lb_ref, o_ref, win_ref, c_ref,
                 *, tt, rows, cols):
    d = cur_ref.shape[1]
    first = pl.program_id(1) == 0
    win_ref[0:CONV_HALO, :] = jnp.where(first, hist_ref[0], prev_ref[...])
    win_ref[CONV_HALO:CONV_HALO + tt, :] = cur_ref[...]
    off = CONV_HALO - CONV_HIST
    for c0 in range(0, d, cols):
        for r0 in range(0, tt, rows):
            acc = jnp.zeros((rows, cols), F32)
            for s in range(SUBLANES):
                ext = rows + (SUBLANES if s else 0)
                part = jnp.zeros((ext // SUBLANES, SUBLANES, cols), F32)
                for k in range(CONV_WIDTH):
                    if (off + k) % SUBLANES != s:
                        continue
                    a = r0 + off + k - s
                    tap = win_ref[a:a + ext, c0:c0 + cols].reshape(ext // SUBLANES, SUBLANES, cols)
                    part = part + tap * w_ref[k * SUBLANES:(k + 1) * SUBLANES, c0:c0 + cols][None]
                acc = acc + part.reshape(ext, cols)[s:s + rows]
            c_ref[r0:r0 + rows, c0:c0 + cols] = acc + bdw_ref[:, c0:c0 + cols]
    c = c_ref[...]
    xc = c - jnp.mean(c, axis=-1, keepdims=True)
    y = xc * lax.rsqrt(jnp.mean(xc * xc, axis=-1, keepdims=True) + EPS) * lg_ref[...] + lb_ref[...]
    o_ref[...] = (y * jax.nn.sigmoid(y)).astype(o_ref.dtype)


def _conv_core(g, hist, w_dw, b_dw, ln_g, ln_b, *, n_batch, tt):
    m, d = g.shape
    t = m // n_batch
    nt = t // tt
    ratio = tt // CONV_HALO if tt >= CONV_HALO else None
    if ratio is None:
        prev_spec = pl.BlockSpec((CONV_HALO, d), lambda b, i: (0, 0))
        prev = hist.reshape(n_batch * CONV_HALO, d)
    else:
        prev_spec = pl.BlockSpec((CONV_HALO, d), lambda b, i: (jnp.maximum((b * nt + i) * ratio - 1, 0), 0))
        prev = g
    w_rep = jnp.repeat(w_dw, SUBLANES, axis=0)
    return pl.pallas_call(
        functools.partial(_conv_kernel, tt=tt, rows=min(tt, 128), cols=HD),
        grid=(n_batch, nt),
        in_specs=[
            pl.BlockSpec((1, CONV_HALO, d), lambda b, i: (b, 0, 0)),
            prev_spec,
            pl.BlockSpec((tt, d), lambda b, i: (b * nt + i, 0)),
            pl.BlockSpec((CONV_WIDTH * SUBLANES, d), lambda b, i: (0, 0)),
            pl.BlockSpec((1, d), lambda b, i: (0, 0)),
            pl.BlockSpec((1, d), lambda b, i: (0, 0)),
            pl.BlockSpec((1, d), lambda b, i: (0, 0)),
        ],
        out_specs=pl.BlockSpec((tt, d), lambda b, i: (b * nt + i, 0)),
        out_shape=jax.ShapeDtypeStruct((m, d), BF16),
        scratch_shapes=[pltpu.VMEM((CONV_HALO + tt, d), F32), pltpu.VMEM((tt, d), F32)],
        compiler_params=_cparams(("parallel", "arbitrary")),
        name="conv_core",
    )(hist, prev, g, w_rep, b_dw.reshape(1, d), ln_g.reshape(1, d), ln_b.reshape(1, d))


def _softmax_block(s, mask):
    s = jnp.where(mask, s, NEG_INF)
    m = jnp.max(s, axis=-1, keepdims=True)
    p = jnp.exp(s - m)
    den = jnp.sum(p, axis=-1, keepdims=True)
    return p, den, m + jnp.log(den)


def _mix_groups(outs, lses):
    m = jnp.maximum(jnp.maximum(lses[0], lses[1]), lses[2])
    es = [jnp.exp(l - m) for l in lses]
    tot = es[0] + es[1] + es[2]
    return (es[0] * outs[0] + es[1] * outs[1] + es[2] * outs[2]) / tot


DIL_UNROLL = 8


def _largest_divisor(n, cap):
    return max(u for u in range(1, cap + 1) if n % u == 0)


def _dil_prompt_kernel(q0, q1, q2, k0, k1, k2, v0, v1, v2, o_ref, og_ref, lse_ref, *, t, qb):
    q_refs, k_refs, v_refs = (q0, q1, q2), (k0, k1, k2), (v0, v1, v2)
    row = lax.broadcasted_iota(jnp.int32, (qb, 2 * qb), 0)
    col = lax.broadcasted_iota(jnp.int32, (qb, 2 * qb), 1)
    band = (col >= row) & (col <= row + qb)
    causal = (lax.broadcasted_iota(jnp.int32, (qb, qb), 1)
              <= lax.broadcasted_iota(jnp.int32, (qb, qb), 0))

    def rows(start, size, d):
        return pl.ds(start, size) if d == 1 else pl.ds(start, size, stride=d)

    for g, (window, d) in enumerate(DIL_GROUPS):
        assert window == qb * d, "a query block plus the previous one must cover the window"
        q_ref, k_ref, v_ref = q_refs[g], k_refs[g], v_refs[g]
        nblk = t // (d * qb)

        def attend(q_start, k_start, nk, mask, g=g, d=d, q_ref=q_ref, k_ref=k_ref, v_ref=v_ref):
            qv = q_ref[0, rows(q_start, qb, d), :].astype(BF16)
            kv = k_ref[0, rows(k_start, nk, d), :].astype(BF16)
            vv = v_ref[0, rows(k_start, nk, d), :].astype(BF16)
            s = lax.dot_general(qv, kv, (((1,), (1,)), ((), ())), preferred_element_type=F32) * DIL_SCALE
            p, den, lse = _softmax_block(s, mask)
            o = jnp.dot(p.astype(BF16), vv, preferred_element_type=F32) / den
            og_ref[g, rows(q_start, qb, d), :] = o
            lse_ref[g, rows(q_start, qb, d), :] = jnp.broadcast_to(lse, (qb, HD))

        def residue(r, carry, d=d, nblk=nblk, attend=attend):
            attend(r, r, qb, causal)

            def later(i, c):
                attend(r + i * qb * d, r + (i - 1) * qb * d, 2 * qb, band)
                return c

            if nblk > 1:
                lax.fori_loop(1, nblk, later, 0, unroll=_largest_divisor(nblk - 1, DIL_UNROLL))
            return carry

        lax.fori_loop(0, d, residue, 0, unroll=_largest_divisor(d, DIL_UNROLL) if nblk == 1 else 1)

    outs = [og_ref[g] for g in range(3)]
    lses = [lse_ref[g] for g in range(3)]
    o_ref[0] = _mix_groups(outs, lses).astype(o_ref.dtype)


def _dil_prompt(qkv, *, n_batch):
    m, n = qkv.shape
    t = m // n_batch
    qkv3 = qkv.reshape(n_batch, t, n)
    nh = len(DIL_GROUPS) * H_DIL

    def spec(sec, g):
        return pl.BlockSpec((1, t, HD), lambda b, h: (b, 0, sec * nh + g * H_DIL + h))

    out = pl.pallas_call(
        functools.partial(_dil_prompt_kernel, t=t, qb=128),
        grid=(n_batch, H_DIL),
        in_specs=[spec(sec, g) for sec in range(3) for g in range(3)],
        out_specs=pl.BlockSpec((1, t, HD), lambda b, h: (b, 0, h)),
        out_shape=jax.ShapeDtypeStruct((n_batch, t, H_DIL * HD), BF16),
        scratch_shapes=[pltpu.VMEM((3, t, HD), F32), pltpu.VMEM((3, t, HD), F32)],
        compiler_params=_cparams(("parallel", "parallel")),
        name="dil_prompt",
    )(*([qkv3] * 9))
    return out.reshape(m, H_DIL * HD)


def _dil_sample_kernel(qkv_ref, ck0, ck1, ck2, cv0, cv1, cv2, o_ref, *, n_new):
    ck_refs, cv_refs = (ck0, ck1, ck2), (cv0, cv1, cv2)
    nq = qkv_ref.shape[1]
    nh = len(DIL_GROUPS) * H_DIL
    n_col = lax.broadcasted_iota(jnp.int32, (nq, 1), 0)
    outs = [[None] * len(DIL_GROUPS) for _ in range(H_DIL)]
    lses = [[None] * len(DIL_GROUPS) for _ in range(H_DIL)]
    for g, (window, d) in enumerate(DIL_GROUPS):
        length = ck_refs[g].shape[1]
        assert length == window and length % d == 0
        k_flat = ck_refs[g].reshape(length * H_DIL, HD)
        v_flat = cv_refs[g].reshape(length * H_DIL, HD)
        n_i = lax.broadcasted_iota(jnp.int32, (nq, length), 0)
        c_i = lax.broadcasted_iota(jnp.int32, (nq, length), 1)
        mask = (c_i >= n_i) & (((c_i - n_i) & (d - 1)) == 0)
        for h in range(H_DIL):
            def cols(sec, g=g, h=h):
                c0 = (sec * nh + g * H_DIL + h) * HD
                return slice(c0, c0 + HD)

            qf = qkv_ref[0, :, cols(0)]
            kc = k_flat[pl.ds(h, length, stride=H_DIL), :].astype(BF16)
            vc = v_flat[pl.ds(h, length, stride=H_DIL), :].astype(BF16)
            s = lax.dot_general(qf.astype(BF16), kc, (((1,), (1,)), ((), ())), preferred_element_type=F32) * DIL_SCALE
            s = jnp.where(mask, s, NEG_INF)
            m = jnp.max(s, axis=-1, keepdims=True)
            new_scores = []
            for j in range(n_new):
                sj = jnp.sum(qf * qkv_ref[0, j:j + 1, cols(1)], axis=-1, keepdims=True) * DIL_SCALE
                sj = jnp.where((n_col >= j) & (((n_col - j) & (d - 1)) == 0), sj, NEG_INF)
                new_scores.append(sj)
                m = jnp.maximum(m, sj)
            p = jnp.exp(s - m)
            den = jnp.sum(p, axis=-1, keepdims=True)
            o = jnp.dot(p.astype(BF16), vc, preferred_element_type=F32)
            for j in range(n_new):
                pj = jnp.exp(new_scores[j] - m)
                den = den + pj
                o = o + pj * qkv_ref[0, j:j + 1, cols(2)]
            outs[h][g] = o / den
            lses[h][g] = m + jnp.log(den)
    for h in range(H_DIL):
        o_ref[0, :, h * HD:(h + 1) * HD] = _mix_groups(outs[h], lses[h]).astype(o_ref.dtype)


def _dil_sample(qkv, caches, layer, *, n_batch, n_new):
    n = qkv.shape[1]
    nq = SUBLANES
    qkv3 = jnp.pad(qkv.reshape(n_batch, n_new, n), ((0, 0), (0, nq - n_new), (0, 0)))
    flat = [c.reshape(c.shape[0] * n_batch, c.shape[2], 2 * H_DIL, HD) for c in caches]
    ck_specs = [pl.BlockSpec((1, c.shape[1], H_DIL, HD), lambda b: (layer * n_batch + b, 0, 0, 0)) for c in flat]
    cv_specs = [pl.BlockSpec((1, c.shape[1], H_DIL, HD), lambda b: (layer * n_batch + b, 0, 1, 0)) for c in flat]
    out = pl.pallas_call(
        functools.partial(_dil_sample_kernel, n_new=n_new),
        grid=(n_batch,),
        in_specs=[pl.BlockSpec((1, nq, n), lambda b: (b, 0, 0))] + ck_specs + cv_specs,
        out_specs=pl.BlockSpec((1, nq, H_DIL * HD), lambda b: (b, 0, 0)),
        out_shape=jax.ShapeDtypeStruct((n_batch, nq, H_DIL * HD), BF16),
        compiler_params=_cparams(("parallel",)),
        name="dil_sample",
    )(qkv3, *flat, *flat)
    return out[:, :n_new].reshape(n_batch * n_new, H_DIL * HD)


def _split_bf16(x):
    hi = x.astype(BF16)
    lo = (x - hi.astype(F32)).astype(BF16)
    return hi, lo


def _sb_prompt_kernel(bias_ref, q_ref, k_ref, v_ref, o_ref, acc_ref, surv_ref, *, qb, heads):
    hp = pl.program_id(1)
    i = pl.program_id(2)
    row = lax.broadcasted_iota(jnp.int32, (qb, qb), 0)
    col = lax.broadcasted_iota(jnp.int32, (qb, qb), 1)
    later = (row > col).astype(BF16)
    later2 = jnp.concatenate([later, later], axis=0)
    strict = col < row
    lanes = [slice(hh * HD, (hh + 1) * HD) for hh in range(heads)]
    biases = [bias_ref[0, hp * heads + hh] for hh in range(heads)]
    qs = [(q_ref[0, :, lanes[hh]] * SB_SCALE).astype(BF16) for hh in range(heads)]

    def blocks(k_start, mask):
        zs = []
        for hh in range(heads):
            kv = k_ref[0, pl.ds(k_start, qb), lanes[hh]].astype(BF16)
            zs.append(lax.dot_general(qs[hh], kv, (((1,), (1,)), ((), ())), preferred_element_type=F32) + biases[hh])
        lss, lnegs, betweens = [], [], []
        for hh in range(heads):
            z = zs[hh]
            t = jnp.log(1.0 + jnp.exp(-jnp.abs(z)))
            ls = jnp.minimum(z, 0.0) - t
            lneg = ls - z
            if mask is not None:
                lneg = jnp.where(mask, lneg, 0.0)
            hi, lo = _split_bf16(lneg)
            betweens.append(jnp.dot(jnp.concatenate([hi, lo], axis=1), later2, preferred_element_type=F32))
            lss.append(ls)
            lnegs.append(lneg)
        for hh in range(heads):
            surv = surv_ref[hh]
            a = jnp.exp(lss[hh] + betweens[hh] + surv)
            if mask is not None:
                a = jnp.where(mask, a, 0.0)
            vv = v_ref[0, pl.ds(k_start, qb), lanes[hh]].astype(BF16)
            acc_ref[hh] += jnp.dot(a.astype(BF16), vv, preferred_element_type=F32)
            surv_ref[hh] = surv + jnp.sum(lnegs[hh], axis=-1, keepdims=True)

    acc_ref[...] = jnp.zeros(acc_ref.shape, F32)
    surv_ref[...] = jnp.zeros(surv_ref.shape, F32)
    blocks(pl.multiple_of(i * qb, qb), strict)

    def older(step, carry):
        blocks(pl.multiple_of((i - 1 - step) * qb, qb), None)
        return carry

    lax.fori_loop(0, i, older, 0)
    for hh in range(heads):
        o_ref[0, :, lanes[hh]] = acc_ref[hh].astype(o_ref.dtype)


def _sb_prompt(qkv, bias, *, n_batch, qb, heads=8):
    m, n = qkv.shape
    t = m // n_batch
    qkv3 = qkv.reshape(n_batch, t, n)
    hg = H_SB // heads
    w = heads * HD
    out = pl.pallas_call(
        functools.partial(_sb_prompt_kernel, qb=qb, heads=heads),
        grid=(n_batch, hg, t // qb),
        in_specs=[
            pl.BlockSpec(memory_space=pltpu.SMEM),
            pl.BlockSpec((1, qb, w), lambda b, h, i: (b, i, h)),
            pl.BlockSpec((1, t, w), lambda b, h, i: (b, 0, hg + h)),
            pl.BlockSpec((1, t, w), lambda b, h, i: (b, 0, 2 * hg + h)),
        ],
        out_specs=pl.BlockSpec((1, qb, w), lambda b, h, i: (b, i, h)),
        out_shape=jax.ShapeDtypeStruct((n_batch, t, H_SB * HD), BF16),
        scratch_shapes=[pltpu.VMEM((heads, qb, HD), F32), pltpu.VMEM((heads, qb, 1), F32)],
        compiler_params=_cparams(("parallel", "parallel", "arbitrary")),
        name="sb_prompt",
    )(bias.reshape(1, H_SB), qkv3, qkv3, qkv3)
    return out.reshape(m, H_SB * HD)


def _sb_sample_kernel(pt_ref, bias_ref, q_ref, *rest, n_new, pages_per_step):
    blocks_per_page = 2 * H_SB // SB_BLOCK_HEADS
    new_refs = rest[:blocks_per_page]
    page_refs = rest[blocks_per_page:blocks_per_page * (1 + pages_per_step)]
    o_ref, acc_ref, surv_ref = rest[blocks_per_page * (1 + pages_per_step):]
    p = pl.program_id(1)
    row = lax.broadcasted_iota(jnp.int32, (PAGE_SIZE, PAGE_SIZE), 0)
    col = lax.broadcasted_iota(jnp.int32, (PAGE_SIZE, PAGE_SIZE), 1)
    later = (row > col).astype(BF16)
    later2 = jnp.concatenate([later, later], axis=0)
    qs = [(q_ref[0, h] * SB_SCALE).astype(BF16) for h in range(H_SB)]

    def head_tile(ref, i):
        flat = ref.reshape(PAGE_SIZE * SB_BLOCK_HEADS, HD)
        return flat[pl.ds(i, PAGE_SIZE, stride=SB_BLOCK_HEADS), :].astype(BF16)

    def page(blocks, mask, surv):
        zs = []
        for h in range(H_SB):
            kh = head_tile(blocks[h // SB_BLOCK_HEADS], h % SB_BLOCK_HEADS)
            z = lax.dot_general(qs[h], kh, (((1,), (1,)), ((), ())), preferred_element_type=F32)
            zs.append(z + bias_ref[0, h])
        z = jnp.concatenate(zs, axis=0)
        t = jnp.log(1.0 + jnp.exp(-jnp.abs(z)))
        ls = jnp.minimum(z, 0.0) - t
        lneg = ls - z
        if mask is not None:
            lneg = jnp.where(mask, lneg, 0.0)
        hi, lo = _split_bf16(lneg)
        between = jnp.dot(jnp.concatenate([hi, lo], axis=1), later2, preferred_element_type=F32)
        a = jnp.exp(ls + between + surv)
        if mask is not None:
            a = jnp.where(mask, a, 0.0)
        outs = []
        v0 = H_SB // SB_BLOCK_HEADS
        for h in range(H_SB):
            vh = head_tile(blocks[v0 + h // SB_BLOCK_HEADS], h % SB_BLOCK_HEADS)
            ah = a[h * SB_Q_SLOTS:(h + 1) * SB_Q_SLOTS].astype(BF16)
            outs.append(jnp.dot(ah, vh, preferred_element_type=F32))
        return jnp.concatenate(outs, axis=0), surv + jnp.sum(lneg, axis=-1, keepdims=True)

    @pl.when(p == 0)
    def _():
        mask = (col < (row & (SB_Q_SLOTS - 1))) & (col < n_new)
        contrib, surv = page(new_refs, mask, jnp.zeros((H_SB * SB_Q_SLOTS, 1), F32))
        acc_ref[...] = contrib
        surv_ref[...] = surv

    surv = surv_ref[...]
    total = acc_ref[...]
    for j in range(pages_per_step):
        contrib, surv = page(page_refs[j * blocks_per_page:(j + 1) * blocks_per_page], None, surv)
        total = total + contrib
    acc_ref[...] = total
    surv_ref[...] = surv

    @pl.when(p == pl.num_programs(1) - 1)
    def _():
        for h in range(H_SB):
            o_ref[0, :, h * HD:(h + 1) * HD] = acc_ref[h * SB_Q_SLOTS:(h + 1) * SB_Q_SLOTS, :].astype(o_ref.dtype)


def _sb_sample(qkv, pools, layer, page_table, bias, *, n_batch, n_new, pages_per_step=8):
    hw = H_SB * HD
    n_pool = pools.shape[1]
    n_pages = page_table.shape[1]
    blocks_per_page = 2 * H_SB // SB_BLOCK_HEADS
    q = qkv[:, :hw].reshape(n_batch, n_new, H_SB, HD).transpose(0, 2, 1, 3)
    q = jnp.pad(q, ((0, 0), (0, 0), (0, SB_Q_SLOTS - n_new), (0, 0)))
    new_kv = jnp.pad(qkv[:, hw:].reshape(n_batch, n_new, 2 * H_SB, HD),
                     ((0, 0), (0, PAGE_SIZE - n_new), (0, 0), (0, 0)))
    pool4 = pools.reshape(pools.shape[0] * n_pool, PAGE_SIZE, 2 * H_SB, HD)
    blk = (1, PAGE_SIZE, SB_BLOCK_HEADS, HD)

    def page_spec(j, hb):
        return pl.BlockSpec(
            blk, lambda b, p, pt: (layer * n_pool + pt[b, n_pages - 1 - (p * pages_per_step + j)], 0, hb, 0))

    grid_spec = pltpu.PrefetchScalarGridSpec(
        num_scalar_prefetch=1,
        grid=(n_batch, n_pages // pages_per_step),
        in_specs=[pl.BlockSpec(memory_space=pltpu.SMEM),
                  pl.BlockSpec((1, H_SB, SB_Q_SLOTS, HD), lambda b, p, pt: (b, 0, 0, 0))]
        + [pl.BlockSpec(blk, lambda b, p, pt, hb=hb: (b, 0, hb, 0)) for hb in range(blocks_per_page)]
        + [page_spec(j, hb) for j in range(pages_per_step) for hb in range(blocks_per_page)],
        out_specs=pl.BlockSpec((1, SB_Q_SLOTS, hw), lambda b, p, pt: (b, 0, 0)),
        scratch_shapes=[pltpu.VMEM((H_SB * SB_Q_SLOTS, HD), F32), pltpu.VMEM((H_SB * SB_Q_SLOTS, 1), F32)],
    )
    out = pl.pallas_call(
        functools.partial(_sb_sample_kernel, n_new=n_new, pages_per_step=pages_per_step),
        grid_spec=grid_spec,
        out_shape=jax.ShapeDtypeStruct((n_batch, SB_Q_SLOTS, hw), BF16),
        compiler_params=_cparams(("parallel", "arbitrary")),
        name="sb_sample",
    )(page_table, bias.reshape(1, H_SB), q, *([new_kv] * blocks_per_page),
      *([pool4] * (pages_per_step * blocks_per_page)))
    return out[:, :n_new].reshape(n_batch * n_new, hw)


def _heads_to_rows_kernel(x_ref, o_ref):
    tm = x_ref.shape[0]
    flat = o_ref.reshape(tm * SUBLANES, HD)
    for c in range(SUBLANES):
        flat[pl.ds(c, tm, stride=SUBLANES), :] = x_ref[:, c * HD:(c + 1) * HD]


def _heads_to_rows(x, *, n_batch, length, first_block, block_stride, n_blocks):
    m, _ = x.shape
    t = m // n_batch
    tm = min(length, 512)
    nt = length // tm
    first = (t - length) // tm
    w = SUBLANES * HD
    return pl.pallas_call(
        _heads_to_rows_kernel,
        grid=(n_batch, nt, n_blocks),
        in_specs=[pl.BlockSpec((tm, w), lambda b, i, j: (b * (t // tm) + first + i, first_block + j * block_stride))],
        out_specs=pl.BlockSpec((tm, SUBLANES, HD), lambda b, i, j: (b * nt + i, j, 0)),
        out_shape=jax.ShapeDtypeStruct((n_batch * length, n_blocks * SUBLANES, HD), F32),
        compiler_params=_cparams(("parallel", "parallel", "parallel")),
        name="heads_to_rows",
    )(x)


def _tiles(m):
    return 1024 if m % 1024 == 0 else m


def _ffn(x, gain, w_gate, w_up, w_down, layer):
    m = x.shape[0]
    d_ff = w_gate.shape[2]
    zeros = jnp.zeros((1, d_ff), F32)
    a = _norm_dual(x, gain, w_gate, w_up, layer, zeros, zeros, act="swiglu", col0=0, col1=0, n_out=d_ff,
                   out_dtype=BF16, tm=_tiles(m), tn=512)
    return _res_matmul(a, w_down, layer, jnp.zeros((x.shape[1],), F32), x)


def _conv_layer(x, gain, hist, w1, b1, w_dw, b_dw, ln_g, ln_b, w2, b2, layer, *, n_batch):
    m, d = x.shape
    t = m // n_batch
    b1r = b1.reshape(1, 2 * d)
    g = _norm_dual(x, gain, w1, w1, layer, b1r, b1r, act="glu", col0=0, col1=d, n_out=d, out_dtype=F32,
                   tm=_tiles(m), tn=512)
    if t >= 256:
        tt, g_pad = 256, g
    else:
        tt = 8
        g_pad = jnp.pad(g.reshape(n_batch, t, d), ((0, 0), (0, tt - t), (0, 0))).reshape(n_batch * tt, d)
    hist_pad = jnp.pad(hist, ((0, 0), (CONV_HALO - CONV_HIST, 0), (0, 0)))
    s = _conv_core(g_pad, hist_pad, w_dw, b_dw, ln_g, ln_b, n_batch=n_batch, tt=tt)
    if tt != 256:
        s = s.reshape(n_batch, tt, d)[:, :t].reshape(m, d)
    x = _res_matmul(s, w2, layer, b2, x)
    state = jnp.concatenate([hist, g.reshape(n_batch, t, d)], axis=1)[:, -CONV_HIST:]
    return x, state


def _dil_qkv(x, gain, w_qkv, layer, q_gain, k_gain):
    m = x.shape[0]
    third = w_qkv.shape[2] // 3
    return _norm_single(x, gain, w_qkv, layer, jnp.stack([q_gain, k_gain]), n_normed_cols=2 * third,
                        tm=_tiles(m), tn=512)


def _dil_states(qkv, n_batch):
    t = qkv.shape[0] // n_batch
    q5 = qkv.reshape(n_batch, t, 3, len(DIL_GROUPS), H_DIL, HD)
    return [jnp.stack([q5[:, :, 1, g], q5[:, :, 2, g]], axis=2) for g in range(len(DIL_GROUPS))]


def kernel(x_prompt, x_sample, state_conv, cache_win_g0, cache_win_g1, cache_win_g2, cache_sb_kv, page_table, ln_mix, ln_ffn, conv_w1, conv_b1, conv_w_dw, conv_b_dw, conv_ln_g, conv_ln_b, conv_w2, conv_b2, dil_w_qkv, dil_q_gain, dil_k_gain, dil_w_o, sb_w_qkv, sb_w_o, sb_bias, ffn_w_gate, ffn_w_up, ffn_w_down):
    bp, tp, d = x_prompt.shape
    bs, ts, _ = x_sample.shape
    depth = ln_mix.shape[0]
    xp = x_prompt.reshape(bp * tp, d)
    xs = x_sample.reshape(bs * ts, d)
    win_caches = (cache_win_g0, cache_win_g1, cache_win_g2)
    conv_p, conv_s, sb_p, sb_s = [], [], [], []
    win_p = [[] for _ in DIL_GROUPS]
    win_s = [[] for _ in DIL_GROUPS]
    i_conv = i_dil = i_sb = 0
    zero_d = jnp.zeros((d,), F32)
    for i in range(depth):
        kind = i % 3
        if kind == 0:
            j = i_conv
            prm = (conv_w1, conv_b1[j], conv_w_dw[j], conv_b_dw[j], conv_ln_g[j], conv_ln_b[j], conv_w2, conv_b2[j], j)
            xp, st_p = _conv_layer(xp, ln_mix[i], jnp.zeros((bp, CONV_HIST, d), F32), *prm, n_batch=bp)
            xs, st_s = _conv_layer(xs, ln_mix[i], state_conv[j], *prm, n_batch=bs)
            conv_p.append(st_p)
            conv_s.append(st_s)
            i_conv += 1
        elif kind == 1:
            j = i_dil
            qkv_p = _dil_qkv(xp, ln_mix[i], dil_w_qkv, j, dil_q_gain[j], dil_k_gain[j])
            qkv_s = _dil_qkv(xs, ln_mix[i], dil_w_qkv, j, dil_q_gain[j], dil_k_gain[j])
            o_p = _dil_prompt(qkv_p, n_batch=bp)
            o_s = _dil_sample(qkv_s, win_caches, j, n_batch=bs, n_new=ts)
            xp = _res_matmul(o_p, dil_w_o, j, zero_d, xp)
            xs = _res_matmul(o_s, dil_w_o, j, zero_d, xs)
            new_s = _dil_states(qkv_s, bs)
            n_g = len(DIL_GROUPS)
            for g, (window, _) in enumerate(DIL_GROUPS):
                keep = min(window, tp)
                st = _heads_to_rows(qkv_p, n_batch=bp, length=keep, first_block=n_g + g, block_stride=n_g, n_blocks=2)
                win_p[g].append(st.reshape(bp, keep, 2, H_DIL, HD))
                length = win_caches[g].shape[2]
                win_s[g].append(jnp.concatenate([win_caches[g][j], new_s[g]], axis=1)[:, -length:])
            i_dil += 1
        else:
            j = i_sb
            hw = H_SB * HD
            ones_gain = jnp.ones((1, HD), F32)
            qkv_p = _norm_single(xp, ln_mix[i], sb_w_qkv, j, ones_gain, n_normed_cols=0, tm=_tiles(xp.shape[0]), tn=512)
            qkv_s = _norm_single(xs, ln_mix[i], sb_w_qkv, j, ones_gain, n_normed_cols=0, tm=_tiles(xs.shape[0]), tn=512)
            o_p = _sb_prompt(qkv_p, sb_bias[j], n_batch=bp, qb=256)
            o_s = _sb_sample(qkv_s, cache_sb_kv, j, page_table, sb_bias[j], n_batch=bs, n_new=ts)
            xp = _res_matmul(o_p, sb_w_o, j, zero_d, xp)
            xs = _res_matmul(o_s, sb_w_o, j, zero_d, xs)
            kv_blocks = 2 * H_SB // SUBLANES
            st = _heads_to_rows(qkv_p, n_batch=bp, length=tp, first_block=kv_blocks // 2, block_stride=1, n_blocks=kv_blocks)
            sb_p.append(st.reshape(bp, tp, 2, H_SB, HD))
            sb_s.append(qkv_s[:, hw:].reshape(bs, ts, 2, H_SB, HD))
            i_sb += 1
        xp = _ffn(xp, ln_ffn[i], ffn_w_gate, ffn_w_up, ffn_w_down, i)
        xs = _ffn(xs, ln_ffn[i], ffn_w_gate, ffn_w_up, ffn_w_down, i)
    return (xp.reshape(bp, tp, d), xs.reshape(bs, ts, d), jnp.stack(conv_p), jnp.stack(conv_s),
            jnp.stack(win_p[0]), jnp.stack(win_s[0]), jnp.stack(win_p[1]), jnp.stack(win_s[1]),
            jnp.stack(win_p[2]), jnp.stack(win_s[2]), jnp.stack(sb_p), jnp.stack(sb_s))
```

```python
import functools

import jax
import jax.numpy as jnp
from jax import lax
from jax.experimental import pallas as pl
from jax.experimental.pallas import tpu as pltpu

F32 = jnp.float32
BF16 = jnp.bfloat16

D_MODEL = 2048
CONV_WIDTH = 31
CONV_HIST = CONV_WIDTH - 1
CONV_HALO = 32
SUBLANES = 8
SB_BLOCK_HEADS = 8
SB_Q_SLOTS = 8
DIL_GROUPS = ((128, 1), (512, 4), (2048, 16))
H_DIL = 8
HD = 128
H_SB = 16
PAGE_SIZE = 128
EPS = 1e-6
NEG_INF = -1e30
DIL_SCALE = HD ** -0.5
SB_SCALE = HD ** -0.5

VMEM_LIMIT_BYTES = 56 * 1024 * 1024
QKV_TN = 1024
RES_TM = (2048, 1024)
RES_A_BYTES = 24 * 1024 * 1024


def _cparams(sem):
    return pltpu.CompilerParams(dimension_semantics=sem, vmem_limit_bytes=VMEM_LIMIT_BYTES)


def _log_sigmoid(z):
    return jnp.minimum(z, 0.0) - jnp.log1p(jnp.exp(-jnp.abs(z)))


def _rmsnorm_rows(x_ref, g_ref, xn_ref, r0, rows):
    x = x_ref[r0:r0 + rows, :]
    ms = jnp.mean(x * x, axis=-1, keepdims=True)
    xn = (x * lax.rsqrt(ms + EPS) * g_ref[...]).astype(BF16)
    xn_ref[r0:r0 + rows, :] = xn
    return xn


def _first_column_split(body):
    j = pl.program_id(1)
    pl.when(j == 0)(lambda: body(True))
    pl.when(j != 0)(lambda: body(False))


def _norm_dual_kernel(x_ref, g_ref, w0_ref, w1_ref, b0_ref, b1_ref, o_ref, xn_ref, *, act, row_chunk):
    def body(first):
        w0 = w0_ref[...].astype(BF16)
        w1 = w1_ref[...].astype(BF16)
        for r0 in range(0, o_ref.shape[0], row_chunk):
            a = _rmsnorm_rows(x_ref, g_ref, xn_ref, r0, row_chunk) if first else xn_ref[r0:r0 + row_chunk, :]
            u0 = jnp.dot(a, w0, preferred_element_type=F32) + b0_ref[...]
            u1 = jnp.dot(a, w1, preferred_element_type=F32) + b1_ref[...]
            if act == "swiglu":
                o_ref[r0:r0 + row_chunk, :] = (u0 * jax.nn.sigmoid(u0) * u1).astype(o_ref.dtype)
            else:
                o_ref[r0:r0 + row_chunk, :] = (u0 * jax.nn.sigmoid(u1)).astype(o_ref.dtype)

    _first_column_split(body)


def _layer_weight_spec(k, tn, layer, col_block=0):
    return pl.BlockSpec((None, k, tn), lambda i, j: (layer, 0, j + col_block))


def _norm_dual(x, gain, w0, w1, layer, b0, b1, *, act, col0, col1, n_out, out_dtype, tm, tn):
    m, k = x.shape
    o0, o1 = col0 // tn, col1 // tn
    return pl.pallas_call(
        functools.partial(_norm_dual_kernel, act=act, row_chunk=min(tm, 256)),
        grid=(m // tm, n_out // tn),
        in_specs=[
            pl.BlockSpec((tm, k), lambda i, j: (i, 0)),
            pl.BlockSpec((1, k), lambda i, j: (0, 0)),
            _layer_weight_spec(k, tn, layer, o0),
            _layer_weight_spec(k, tn, layer, o1),
            pl.BlockSpec((1, tn), lambda i, j: (0, j + o0)),
            pl.BlockSpec((1, tn), lambda i, j: (0, j + o1)),
        ],
        out_specs=pl.BlockSpec((tm, tn), lambda i, j: (i, j)),
        out_shape=jax.ShapeDtypeStruct((m, n_out), out_dtype),
        scratch_shapes=[pltpu.VMEM((tm, k), BF16)],
        compiler_params=_cparams(("parallel", "arbitrary")),
        name=f"norm_dual_{act}",
    )(x, gain.reshape(1, k), w0, w1, b0, b1)


def _norm_single_kernel(x_ref, g_ref, w_ref, hg_ref, o_ref, xn_ref, *, n_normed_tiles, row_chunk):
    j = pl.program_id(1)
    tm, tn = o_ref.shape

    def body(first):
        wb = w_ref[...].astype(BF16)
        for r0 in range(0, tm, row_chunk):
            a = _rmsnorm_rows(x_ref, g_ref, xn_ref, r0, row_chunk) if first else xn_ref[r0:r0 + row_chunk, :]
            acc = jnp.dot(a, wb, preferred_element_type=F32)
            if n_normed_tiles:
                hg = hg_ref[0]
                normed = []
                for c in range(tn // HD):
                    blk = acc[:, c * HD:(c + 1) * HD]
                    ms = jnp.mean(blk * blk, axis=-1, keepdims=True)
                    normed.append(blk * lax.rsqrt(ms + EPS) * hg)
                acc = jnp.where(j < n_normed_tiles, jnp.concatenate(normed, axis=1), acc)
            o_ref[r0:r0 + row_chunk, :] = acc

    _first_column_split(body)


def _norm_single(x, gain, w, layer, head_gains, *, n_normed_cols, tm, tn):
    m, k = x.shape
    n = w.shape[2]
    n_normed_tiles = n_normed_cols // tn
    n_sec = head_gains.shape[0]
    tiles_per_sec = max(n_normed_tiles // n_sec, 1)
    return pl.pallas_call(
        functools.partial(_norm_single_kernel, n_normed_tiles=n_normed_tiles, row_chunk=min(tm, 256)),
        grid=(m // tm, n // tn),
        in_specs=[
            pl.BlockSpec((tm, k), lambda i, j: (i, 0)),
            pl.BlockSpec((1, k), lambda i, j: (0, 0)),
            _layer_weight_spec(k, tn, layer),
            pl.BlockSpec((1, 1, HD), lambda i, j: (jnp.minimum(j // tiles_per_sec, n_sec - 1), 0, 0)),
        ],
        out_specs=pl.BlockSpec((tm, tn), lambda i, j: (i, j)),
        out_shape=jax.ShapeDtypeStruct((m, n), F32),
        scratch_shapes=[pltpu.VMEM((tm, k), BF16)],
        compiler_params=_cparams(("parallel", "arbitrary")),
        name="norm_single",
    )(x, gain.reshape(1, k), w, head_gains.reshape(n_sec, 1, HD))


def _res_kernel(a_ref, w_ref, b_ref, r_ref, o_ref):
    acc = jnp.dot(a_ref[...], w_ref[...].astype(BF16), preferred_element_type=F32)
    o_ref[...] = r_ref[...] + (acc + b_ref[...])


def _res_matmul(a, w, layer, bias, res):
    m, k = a.shape
    n = w.shape[2]
    tm = next((t for t in RES_TM if m % t == 0 and 2 * t * k * a.dtype.itemsize <= RES_A_BYTES), m)
    tn = 512 if k <= 2048 else 256
    return pl.pallas_call(
        _res_kernel,
        grid=(m // tm, n // tn),
        in_specs=[
            pl.BlockSpec((tm, k), lambda i, j: (i, 0)),
            _layer_weight_spec(k, tn, layer),
            pl.BlockSpec((1, tn), lambda i, j: (0, j)),
            pl.BlockSpec((tm, tn), lambda i, j: (i, j)),
        ],
        out_specs=pl.BlockSpec((tm, tn), lambda i, j: (i, j)),
        out_shape=jax.ShapeDtypeStruct((m, n), F32),
        compiler_params=_cparams(("parallel", "arbitrary")),
        name="res_matmul",
    )(a, w, bias.reshape(1, n), res)


def _conv_kernel(hist_ref, prev_ref, cur_ref, w_ref, bdw_ref, lg_ref, lb_ref, o_ref, win_ref, c_ref,
                 *, tt, rows, cols):
    d = cur_ref.shape[1]
    first = pl.program_id(1) == 0
    win_ref[0:CONV_HALO, :] = jnp.where(first, hist_ref[0], prev_ref[...])
    win_ref[CONV_HALO:CONV_HALO + tt, :] = cur_ref[...]
    off = CONV_HALO - CONV_HIST
    for c0 in range(0, d, cols):
        for r0 in range(0, tt, rows):
            acc = jnp.zeros((rows, cols), F32)
            for s in range(SUBLANES):
                ext = rows + (SUBLANES if s else 0)
                part = jnp.zeros((ext // SUBLANES, SUBLANES, cols), F32)
                for k in range(CONV_WIDTH):
                    if (off + k) % SUBLANES != s:
                        continue
                    a = r0 + off + k - s
                    tap = win_ref[a:a + ext, c0:c0 + cols].reshape(ext // SUBLANES, SUBLANES, cols)
                    part = part + tap * w_ref[k * SUBLANES:(k + 1) * SUBLANES, c0:c0 + cols][None]
                acc = acc + part.reshape(ext, cols)[s:s + rows]
            c_ref[r0:r0 + rows, c0:c0 + cols] = acc + bdw_ref[:, c0:c0 + cols]
    c = c_ref[...]
    xc = c - jnp.mean(c, axis=-1, keepdims=True)
    y = xc * lax.rsqrt(jnp.mean(xc * xc, axis=-1, keepdims=True) + EPS) * lg_ref[...] + lb_ref[...]
    o_ref[...] = (y * jax.nn.sigmoid(y)).astype(o_ref.dtype)


def _conv_core(g, hist, w_dw, b_dw, ln_g, ln_b, *, n_batch, tt):
    m, d = g.shape
    t = m // n_batch
    nt = t // tt
    ratio = tt // CONV_HALO if tt >= CONV_HALO else None
    if ratio is None:
        prev_spec = pl.BlockSpec((CONV_HALO, d), lambda b, i: (0, 0))
        prev = hist.reshape(n_batch * CONV_HALO, d)
    else:
        prev_spec = pl.BlockSpec((CONV_HALO, d), lambda b, i: (jnp.maximum((b * nt + i) * ratio - 1, 0), 0))
        prev = g
    w_rep = jnp.repeat(w_dw, SUBLANES, axis=0)
    return pl.pallas_call(
        functools.partial(_conv_kernel, tt=tt, rows=min(tt, 128), cols=HD),
        grid=(n_batch, nt),
        in_specs=[
            pl.BlockSpec((1, CONV_HALO, d), lambda b, i: (b, 0, 0)),
            prev_spec,
            pl.BlockSpec((tt, d), lambda b, i: (b * nt + i, 0)),
            pl.BlockSpec((CONV_WIDTH * SUBLANES, d), lambda b, i: (0, 0)),
            pl.BlockSpec((1, d), lambda b, i: (0, 0)),
            pl.BlockSpec((1, d), lambda b, i: (0, 0)),
            pl.BlockSpec((1, d), lambda b, i: (0, 0)),
        ],
        out_specs=pl.BlockSpec((tt, d), lambda b, i: (b * nt + i, 0)),
        out_shape=jax.ShapeDtypeStruct((m, d), BF16),
        scratch_shapes=[pltpu.VMEM((CONV_HALO + tt, d), F32), pltpu.VMEM((tt, d), F32)],
        compiler_params=_cparams(("parallel", "arbitrary")),
        name="conv_core",
    )(hist, prev, g, w_rep, b_dw.reshape(1, d), ln_g.reshape(1, d), ln_b.reshape(1, d))


def _softmax_block(s, mask):
    s = jnp.where(mask, s, NEG_INF)
    m = jnp.max(s, axis=-1, keepdims=True)
    p = jnp.exp(s - m)
    den = jnp.sum(p, axis=-1, keepdims=True)
    return p, den, m + jnp.log(den)


def _mix_groups(outs, lses):
    m = jnp.maximum(jnp.maximum(lses[0], lses[1]), lses[2])
    es = [jnp.exp(l - m) for l in lses]
    tot = es[0] + es[1] + es[2]
    return (es[0] * outs[0] + es[1] * outs[1] + es[2] * outs[2]) / tot


DIL_UNROLL = 8


def _largest_divisor(n, cap):
    return max(u for u in range(1, cap + 1) if n % u == 0)


def _dil_prompt_kernel(q0, q1, q2, k0, k1, k2, v0, v1, v2, o_ref, og_ref, lse_ref, *, t, qb):
    q_refs, k_refs, v_refs = (q0, q1, q2), (k0, k1, k2), (v0, v1, v2)
    row = lax.broadcasted_iota(jnp.int32, (qb, 2 * qb), 0)
    col = lax.broadcasted_iota(jnp.int32, (qb, 2 * qb), 1)
    band = (col >= row) & (col <= row + qb)
    causal = (lax.broadcasted_iota(jnp.int32, (qb, qb), 1)
              <= lax.broadcasted_iota(jnp.int32, (qb, qb), 0))

    def rows(start, size, d):
        return pl.ds(start, size) if d == 1 else pl.ds(start, size, stride=d)

    for g, (window, d) in enumerate(DIL_GROUPS):
        assert window == qb * d, "a query block plus the previous one must cover the window"
        q_ref, k_ref, v_ref = q_refs[g], k_refs[g], v_refs[g]
        nblk = t // (d * qb)

        def attend(q_start, k_start, nk, mask, g=g, d=d, q_ref=q_ref, k_ref=k_ref, v_ref=v_ref):
            qv = q_ref[0, rows(q_start, qb, d), :].astype(BF16)
            kv = k_ref[0, rows(k_start, nk, d), :].astype(BF16)
            vv = v_ref[0, rows(k_start, nk, d), :].astype(BF16)
            s = lax.dot_general(qv, kv, (((1,), (1,)), ((), ())), preferred_element_type=F32) * DIL_SCALE
            p, den, lse = _softmax_block(s, mask)
            o = jnp.dot(p.astype(BF16), vv, preferred_element_type=F32) / den
            og_ref[g, rows(q_start, qb, d), :] = o
            lse_ref[g, rows(q_start, qb, d), :] = jnp.broadcast_to(lse, (qb, HD))

        def residue(r, carry, d=d, nblk=nblk, attend=attend):
            attend(r, r, qb, causal)

            def later(i, c):
                attend(r + i * qb * d, r + (i - 1) * qb * d, 2 * qb, band)
                return c

            if nblk > 1:
                lax.fori_loop(1, nblk, later, 0, unroll=_largest_divisor(nblk - 1, DIL_UNROLL))
            return carry

        lax.fori_loop(0, d, residue, 0, unroll=_largest_divisor(d, DIL_UNROLL) if nblk == 1 else 1)

    outs = [og_ref[g] for g in range(3)]
    lses = [lse_ref[g] for g in range(3)]
    o_ref[0] = _mix_groups(outs, lses).astype(o_ref.dtype)


def _dil_prompt(qkv, *, n_batch):
    m, n = qkv.shape
    t = m // n_batch
    qkv3 = qkv.reshape(n_batch, t, n)
    nh = len(DIL_GROUPS) * H_DIL

    def spec(sec, g):
        return pl.BlockSpec((1, t, HD), lambda b, h: (b, 0, sec * nh + g * H_DIL + h))

    out = pl.pallas_call(
        functools.partial(_dil_prompt_kernel, t=t, qb=128),
        grid=(n_batch, H_DIL),
        in_specs=[spec(sec, g) for sec in range(3) for g in range(3)],
        out_specs=pl.BlockSpec((1, t, HD), lambda b, h: (b, 0, h)),
        out_shape=jax.ShapeDtypeStruct((n_batch, t, H_DIL * HD), BF16),
        scratch_shapes=[pltpu.VMEM((3, t, HD), F32), pltpu.VMEM((3, t, HD), F32)],
        compiler_params=_cparams(("parallel", "parallel")),
        name="dil_prompt",
    )(*([qkv3] * 9))
    return out.reshape(m, H_DIL * HD)


def _dil_sample_kernel(qkv_ref, ck0, ck1, ck2, cv0, cv1, cv2, o_ref, *, n_new):
    ck_refs, cv_refs = (ck0, ck1, ck2), (cv0, cv1, cv2)
    nq = qkv_ref.shape[1]
    nh = len(DIL_GROUPS) * H_DIL
    n_col = lax.broadcasted_iota(jnp.int32, (nq, 1), 0)
    outs = [[None] * len(DIL_GROUPS) for _ in range(H_DIL)]
    lses = [[None] * len(DIL_GROUPS) for _ in range(H_DIL)]
    for g, (window, d) in enumerate(DIL_GROUPS):
        length = ck_refs[g].shape[1]
        assert length == window and length % d == 0
        k_flat = ck_refs[g].reshape(length * H_DIL, HD)
        v_flat = cv_refs[g].reshape(length * H_DIL, HD)
        n_i = lax.broadcasted_iota(jnp.int32, (nq, length), 0)
        c_i = lax.broadcasted_iota(jnp.int32, (nq, length), 1)
        mask = (c_i >= n_i) & (((c_i - n_i) & (d - 1)) == 0)
        for h in range(H_DIL):
            def cols(sec, g=g, h=h):
                c0 = (sec * nh + g * H_DIL + h) * HD
                return slice(c0, c0 + HD)

            qf = qkv_ref[0, :, cols(0)]
            kc = k_flat[pl.ds(h, length, stride=H_DIL), :].astype(BF16)
            vc = v_flat[pl.ds(h, length, stride=H_DIL), :].astype(BF16)
            s = lax.dot_general(qf.astype(BF16), kc, (((1,), (1,)), ((), ())), preferred_element_type=F32) * DIL_SCALE
            s = jnp.where(mask, s, NEG_INF)
            m = jnp.max(s, axis=-1, keepdims=True)
            new_scores = []
            for j in range(n_new):
                sj = jnp.sum(qf * qkv_ref[0, j:j + 1, cols(1)], axis=-1, keepdims=True) * DIL_SCALE
                sj = jnp.where((n_col >= j) & (((n_col - j) & (d - 1)) == 0), sj, NEG_INF)
                new_scores.append(sj)
                m = jnp.maximum(m, sj)
            p = jnp.exp(s - m)
            den = jnp.sum(p, axis=-1, keepdims=True)
            o = jnp.dot(p.astype(BF16), vc, preferred_element_type=F32)
            for j in range(n_new):
                pj = jnp.exp(new_scores[j] - m)
                den = den + pj
                o = o + pj * qkv_ref[0, j:j + 1, cols(2)]
            outs[h][g] = o / den
            lses[h][g] = m + jnp.log(den)
    for h in range(H_DIL):
        o_ref[0, :, h * HD:(h + 1) * HD] = _mix_groups(outs[h], lses[h]).astype(o_ref.dtype)


def _dil_sample(qkv, caches, layer, *, n_batch, n_new):
    n = qkv.shape[1]
    nq = SUBLANES
    qkv3 = jnp.pad(qkv.reshape(n_batch, n_new, n), ((0, 0), (0, nq - n_new), (0, 0)))
    flat = [c.reshape(c.shape[0] * n_batch, c.shape[2], 2 * H_DIL, HD) for c in caches]
    ck_specs = [pl.BlockSpec((1, c.shape[1], H_DIL, HD), lambda b: (layer * n_batch + b, 0, 0, 0)) for c in flat]
    cv_specs = [pl.BlockSpec((1, c.shape[1], H_DIL, HD), lambda b: (layer * n_batch + b, 0, 1, 0)) for c in flat]
    out = pl.pallas_call(
        functools.partial(_dil_sample_kernel, n_new=n_new),
        grid=(n_batch,),
        in_specs=[pl.BlockSpec((1, nq, n), lambda b: (b, 0, 0))] + ck_specs + cv_specs,
        out_specs=pl.BlockSpec((1, nq, H_DIL * HD), lambda b: (b, 0, 0)),
        out_shape=jax.ShapeDtypeStruct((n_batch, nq, H_DIL * HD), BF16),
        compiler_params=_cparams(("parallel",)),
        name="dil_sample",
    )(qkv3, *flat, *flat)
    return out[:, :n_new].reshape(n_batch * n_new, H_DIL * HD)


def _split_bf16(x):
    hi = x.astype(BF16)
    lo = (x - hi.astype(F32)).astype(BF16)
    return hi, lo


def _sb_prompt_kernel(bias_ref, q_ref, k_ref, v_ref, o_ref, acc_ref, surv_ref, *, qb, heads):
    hp = pl.program_id(1)
    i = pl.program_id(2)
    row = lax.broadcasted_iota(jnp.int32, (qb, qb), 0)
    col = lax.broadcasted_iota(jnp.int32, (qb, qb), 1)
    later = (row > col).astype(BF16)
    later2 = jnp.concatenate([later, later], axis=0)
    strict = col < row
    lanes = [slice(hh * HD, (hh + 1) * HD) for hh in range(heads)]
    biases = [bias_ref[0, hp * heads + hh] for hh in range(heads)]
    qs = [(q_ref[0, :, lanes[hh]] * SB_SCALE).astype(BF16) for hh in range(heads)]

    def blocks(k_start, mask):
        zs = []
        for hh in range(heads):
            kv = k_ref[0, pl.ds(k_start, qb), lanes[hh]].astype(BF16)
            zs.append(lax.dot_general(qs[hh], kv, (((1,), (1,)), ((), ())), preferred_element_type=F32) + biases[hh])
        lss, lnegs, betweens = [], [], []
        for hh in range(heads):
            z = zs[hh]
            t = jnp.log(1.0 + jnp.exp(-jnp.abs(z)))
            ls = jnp.minimum(z, 0.0) - t
            lneg = ls - z
            if mask is not None:
                lneg = jnp.where(mask, lneg, 0.0)
            hi, lo = _split_bf16(lneg)
            betweens.append(jnp.dot(jnp.concatenate([hi, lo], axis=1), later2, preferred_element_type=F32))
            lss.append(ls)
            lnegs.append(lneg)
        for hh in range(heads):
            surv = surv_ref[hh]
            a = jnp.exp(lss[hh] + betweens[hh] + surv)
            if mask is not None:
                a = jnp.where(mask, a, 0.0)
            vv = v_ref[0, pl.ds(k_start, qb), lanes[hh]].astype(BF16)
            acc_ref[hh] += jnp.dot(a.astype(BF16), vv, preferred_element_type=F32)
            surv_ref[hh] = surv + jnp.sum(lnegs[hh], axis=-1, keepdims=True)

    acc_ref[...] = jnp.zeros(acc_ref.shape, F32)
    surv_ref[...] = jnp.zeros(surv_ref.shape, F32)
    blocks(pl.multiple_of(i * qb, qb), strict)

    def older(step, carry):
        blocks(pl.multiple_of((i - 1 - step) * qb, qb), None)
        return carry

    lax.fori_loop(0, i, older, 0)
    for hh in range(heads):
        o_ref[0, :, lanes[hh]] = acc_ref[hh].astype(o_ref.dtype)


def _sb_prompt(qkv, bias, *, n_batch, qb, heads=8):
    m, n = qkv.shape
    t = m // n_batch
    qkv3 = qkv.reshape(n_batch, t, n)
    hg = H_SB // heads
    w = heads * HD
    out = pl.pallas_call(
        functools.partial(_sb_prompt_kernel, qb=qb, heads=heads),
        grid=(n_batch, hg, t // qb),
        in_specs=[
            pl.BlockSpec(memory_space=pltpu.SMEM),
            pl.BlockSpec((1, qb, w), lambda b, h, i: (b, i, h)),
            pl.BlockSpec((1, t, w), lambda b, h, i: (b, 0, hg + h)),
            pl.BlockSpec((1, t, w), lambda b, h, i: (b, 0, 2 * hg + h)),
        ],
        out_specs=pl.BlockSpec((1, qb, w), lambda b, h, i: (b, i, h)),
        out_shape=jax.ShapeDtypeStruct((n_batch, t, H_SB * HD), BF16),
        scratch_shapes=[pltpu.VMEM((heads, qb, HD), F32), pltpu.VMEM((heads, qb, 1), F32)],
        compiler_params=_cparams(("parallel", "parallel", "arbitrary")),
        name="sb_prompt",
    )(bias.reshape(1, H_SB), qkv3, qkv3, qkv3)
    return out.reshape(m, H_SB * HD)


def _sb_sample_kernel(pt_ref, bias_ref, q_ref, *rest, n_new, pages_per_step):
    blocks_per_page = 2 * H_SB // SB_BLOCK_HEADS
    new_refs = rest[:blocks_per_page]
    page_refs = rest[blocks_per_page:blocks_per_page * (1 + pages_per_step)]
    o_ref, acc_ref, surv_ref = rest[blocks_per_page * (1 + pages_per_step):]
    p = pl.program_id(1)
    row = lax.broadcasted_iota(jnp.int32, (PAGE_SIZE, PAGE_SIZE), 0)
    col = lax.broadcasted_iota(jnp.int32, (PAGE_SIZE, PAGE_SIZE), 1)
    later = (row > col).astype(BF16)
    later2 = jnp.concatenate([later, later], axis=0)
    qs = [(q_ref[0, h] * SB_SCALE).astype(BF16) for h in range(H_SB)]

    def head_tile(ref, i):
        flat = ref.reshape(PAGE_SIZE * SB_BLOCK_HEADS, HD)
        return flat[pl.ds(i, PAGE_SIZE, stride=SB_BLOCK_HEADS), :].astype(BF16)

    def page(blocks, mask, surv):
        zs = []
        for h in range(H_SB):
            kh = head_tile(blocks[h // SB_BLOCK_HEADS], h % SB_BLOCK_HEADS)
            z = lax.dot_general(qs[h], kh, (((1,), (1,)), ((), ())), preferred_element_type=F32)
            zs.append(z + bias_ref[0, h])
        z = jnp.concatenate(zs, axis=0)
        t = jnp.log(1.0 + jnp.exp(-jnp.abs(z)))
        ls = jnp.minimum(z, 0.0) - t
        lneg = ls - z
        if mask is not None:
            lneg = jnp.where(mask, lneg, 0.0)
        hi, lo = _split_bf16(lneg)
        between = jnp.dot(jnp.concatenate([hi, lo], axis=1), later2, preferred_element_type=F32)
        a = jnp.exp(ls + between + surv)
        if mask is not None:
            a = jnp.where(mask, a, 0.0)
        outs = []
        v0 = H_SB // SB_BLOCK_HEADS
        for h in range(H_SB):
            vh = head_tile(blocks[v0 + h // SB_BLOCK_HEADS], h % SB_BLOCK_HEADS)
            ah = a[h * SB_Q_SLOTS:(h + 1) * SB_Q_SLOTS].astype(BF16)
            outs.append(jnp.dot(ah, vh, preferred_element_type=F32))
        return jnp.concatenate(outs, axis=0), surv + jnp.sum(lneg, axis=-1, keepdims=True)

    @pl.when(p == 0)
    def _():
        mask = (col < (row & (SB_Q_SLOTS - 1))) & (col < n_new)
        contrib, surv = page(new_refs, mask, jnp.zeros((H_SB * SB_Q_SLOTS, 1), F32))
        acc_ref[...] = contrib
        surv_ref[...] = surv

    surv = surv_ref[...]
    total = acc_ref[...]
    for j in range(pages_per_step):
        contrib, surv = page(page_refs[j * blocks_per_page:(j + 1) * blocks_per_page], None, surv)
        total = total + contrib
    acc_ref[...] = total
    surv_ref[...] = surv

    @pl.when(p == pl.num_programs(1) - 1)
    def _():
        for h in range(H_SB):
            o_ref[0, :, h * HD:(h + 1) * HD] = acc_ref[h * SB_Q_SLOTS:(h + 1) * SB_Q_SLOTS, :].astype(o_ref.dtype)


def _sb_sample(qkv, pools, layer, page_table, bias, *, n_batch, n_new, pages_per_step=8):
    hw = H_SB * HD
    n_pool = pools.shape[1]
    n_pages = page_table.shape[1]
    blocks_per_page = 2 * H_SB // SB_BLOCK_HEADS
    q = qkv[:, :hw].reshape(n_batch, n_new, H_SB, HD).transpose(0, 2, 1, 3)
    q = jnp.pad(q, ((0, 0), (0, 0), (0, SB_Q_SLOTS - n_new), (0, 0)))
    new_kv = jnp.pad(qkv[:, hw:].reshape(n_batch, n_new, 2 * H_SB, HD),
                     ((0, 0), (0, PAGE_SIZE - n_new), (0, 0), (0, 0)))
    pool4 = pools.reshape(pools.shape[0] * n_pool, PAGE_SIZE, 2 * H_SB, HD)
    blk = (1, PAGE_SIZE, SB_BLOCK_HEADS, HD)

    def page_spec(j, hb):
        return pl.BlockSpec(
            blk, lambda b, p, pt: (layer * n_pool + pt[b, n_pages - 1 - (p * pages_per_step + j)], 0, hb, 0))

    grid_spec = pltpu.PrefetchScalarGridSpec(
        num_scalar_prefetch=1,
        grid=(n_batch, n_pages // pages_per_step),
        in_specs=[pl.BlockSpec(memory_space=pltpu.SMEM),
                  pl.BlockSpec((1, H_SB, SB_Q_SLOTS, HD), lambda b, p, pt: (b, 0, 0, 0))]
        + [pl.BlockSpec(blk, lambda b, p, pt, hb=hb: (b, 0, hb, 0)) for hb in range(blocks_per_page)]
        + [page_spec(j, hb) for j in range(pages_per_step) for hb in range(blocks_per_page)],
        out_specs=pl.BlockSpec((1, SB_Q_SLOTS, hw), lambda b, p, pt: (b, 0, 0)),
        scratch_shapes=[pltpu.VMEM((H_SB * SB_Q_SLOTS, HD), F32), pltpu.VMEM((H_SB * SB_Q_SLOTS, 1), F32)],
    )
    out = pl.pallas_call(
        functools.partial(_sb_sample_kernel, n_new=n_new, pages_per_step=pages_per_step),
        grid_spec=grid_spec,
        out_shape=jax.ShapeDtypeStruct((n_batch, SB_Q_SLOTS, hw), BF16),
        compiler_params=_cparams(("parallel", "arbitrary")),
        name="sb_sample",
    )(page_table, bias.reshape(1, H_SB), q, *([new_kv] * blocks_per_page),
      *([pool4] * (pages_per_step * blocks_per_page)))
    return out[:, :n_new].reshape(n_batch * n_new, hw)


def _heads_to_rows_kernel(x_ref, o_ref):
    tm = x_ref.shape[0]
    flat = o_ref.reshape(tm * SUBLANES, HD)
    for c in range(SUBLANES):
        flat[pl.ds(c, tm, stride=SUBLANES), :] = x_ref[:, c * HD:(c + 1) * HD]


def _heads_to_rows(x, *, n_batch, length, first_block, block_stride, n_blocks):
    m, _ = x.shape
    t = m // n_batch
    tm = min(length, 512)
    nt = length // tm
    first = (t - length) // tm
    w = SUBLANES * HD
    return pl.pallas_call(
        _heads_to_rows_kernel,
        grid=(n_batch, nt, n_blocks),
        in_specs=[pl.BlockSpec((tm, w), lambda b, i, j: (b * (t // tm) + first + i, first_block + j * block_stride))],
        out_specs=pl.BlockSpec((tm, SUBLANES, HD), lambda b, i, j: (b * nt + i, j, 0)),
        out_shape=jax.ShapeDtypeStruct((n_batch * length, n_blocks * SUBLANES, HD), F32),
        compiler_params=_cparams(("parallel", "parallel", "parallel")),
        name="heads_to_rows",
    )(x)


def _tiles(m):
    return 1024 if m % 1024 == 0 else m


def _ffn(x, gain, w_gate, w_up, w_down, layer):
    m = x.shape[0]
    d_ff = w_gate.shape[2]
    zeros = jnp.zeros((1, d_ff), F32)
    a = _norm_dual(x, gain, w_gate, w_up, layer, zeros, zeros, act="swiglu", col0=0, col1=0, n_out=d_ff,
                   out_dtype=BF16, tm=_tiles(m), tn=512)
    return _res_matmul(a, w_down, layer, jnp.zeros((x.shape[1],), F32), x)


def _conv_layer(x, gain, hist, w1, b1, w_dw, b_dw, ln_g, ln_b, w2, b2, layer, *, n_batch):
    m, d = x.shape
    t = m // n_batch
    b1r = b1.reshape(1, 2 * d)
    g = _norm_dual(x, gain, w1, w1, layer, b1r, b1r, act="glu", col0=0, col1=d, n_out=d, out_dtype=F32,
                   tm=_tiles(m), tn=512)
    if t >= 256:
        tt, g_pad = 256, g
    else:
        tt = 8
        g_pad = jnp.pad(g.reshape(n_batch, t, d), ((0, 0), (0, tt - t), (0, 0))).reshape(n_batch * tt, d)
    hist_pad = jnp.pad(hist, ((0, 0), (CONV_HALO - CONV_HIST, 0), (0, 0)))
    s = _conv_core(g_pad, hist_pad, w_dw, b_dw, ln_g, ln_b, n_batch=n_batch, tt=tt)
    if tt != 256:
        s = s.reshape(n_batch, tt, d)[:, :t].reshape(m, d)
    x = _res_matmul(s, w2, layer, b2, x)
    state = jnp.concatenate([hist, g.reshape(n_batch, t, d)], axis=1)[:, -CONV_HIST:]
    return x, state


def _dil_qkv(x, gain, w_qkv, layer, q_gain, k_gain):
    m = x.shape[0]
    third = w_qkv.shape[2] // 3
    return _norm_single(x, gain, w_qkv, layer, jnp.stack([q_gain, k_gain]), n_normed_cols=2 * third,
                        tm=_tiles(m), tn=QKV_TN)


def _dil_states(qkv, n_batch):
    t = qkv.shape[0] // n_batch
    q5 = qkv.reshape(n_batch, t, 3, len(DIL_GROUPS), H_DIL, HD)
    return [jnp.stack([q5[:, :, 1, g], q5[:, :, 2, g]], axis=2) for g in range(len(DIL_GROUPS))]


def kernel(x_prompt, x_sample, state_conv, cache_win_g0, cache_win_g1, cache_win_g2, cache_sb_kv, page_table, ln_mix, ln_ffn, conv_w1, conv_b1, conv_w_dw, conv_b_dw, conv_ln_g, conv_ln_b, conv_w2, conv_b2, dil_w_qkv, dil_q_gain, dil_k_gain, dil_w_o, sb_w_qkv, sb_w_o, sb_bias, ffn_w_gate, ffn_w_up, ffn_w_down):
    bp, tp, d = x_prompt.shape
    bs, ts, _ = x_sample.shape
    depth = ln_mix.shape[0]
    xp = x_prompt.reshape(bp * tp, d)
    xs = x_sample.reshape(bs * ts, d)
    win_caches = (cache_win_g0, cache_win_g1, cache_win_g2)
    conv_p, conv_s, sb_p, sb_s = [], [], [], []
    win_p = [[] for _ in DIL_GROUPS]
    win_s = [[] for _ in DIL_GROUPS]
    i_conv = i_dil = i_sb = 0
    zero_d = jnp.zeros((d,), F32)
    for i in range(depth):
        kind = i % 3
        if kind == 0:
            j = i_conv
            prm = (conv_w1, conv_b1[j], conv_w_dw[j], conv_b_dw[j], conv_ln_g[j], conv_ln_b[j], conv_w2, conv_b2[j], j)
            xp, st_p = _conv_layer(xp, ln_mix[i], jnp.zeros((bp, CONV_HIST, d), F32), *prm, n_batch=bp)
            xs, st_s = _conv_layer(xs, ln_mix[i], state_conv[j], *prm, n_batch=bs)
            conv_p.append(st_p)
            conv_s.append(st_s)
            i_conv += 1
        elif kind == 1:
            j = i_dil
            qkv_p = _dil_qkv(xp, ln_mix[i], dil_w_qkv, j, dil_q_gain[j], dil_k_gain[j])
            qkv_s = _dil_qkv(xs, ln_mix[i], dil_w_qkv, j, dil_q_gain[j], dil_k_gain[j])
            o_p = _dil_prompt(qkv_p, n_batch=bp)
            o_s = _dil_sample(qkv_s, win_caches, j, n_batch=bs, n_new=ts)
            xp = _res_matmul(o_p, dil_w_o, j, zero_d, xp)
            xs = _res_matmul(o_s, dil_w_o, j, zero_d, xs)
            new_s = _dil_states(qkv_s, bs)
            n_g = len(DIL_GROUPS)
            for g, (window, _) in enumerate(DIL_GROUPS):
                keep = min(window, tp)
                st = _heads_to_rows(qkv_p, n_batch=bp, length=keep, first_block=n_g + g, block_stride=n_g, n_blocks=2)
                win_p[g].append(st.reshape(bp, keep, 2, H_DIL, HD))
                length = win_caches[g].shape[2]
                win_s[g].append(jnp.concatenate([win_caches[g][j], new_s[g]], axis=1)[:, -length:])
            i_dil += 1
        else:
            j = i_sb
            hw = H_SB * HD
            ones_gain = jnp.ones((1, HD), F32)
            qkv_p = _norm_single(xp, ln_mix[i], sb_w_qkv, j, ones_gain, n_normed_cols=0, tm=_tiles(xp.shape[0]), tn=QKV_TN)
            qkv_s = _norm_single(xs, ln_mix[i], sb_w_qkv, j, ones_gain, n_normed_cols=0, tm=_tiles(xs.shape[0]), tn=QKV_TN)
            o_p = _sb_prompt(qkv_p, sb_bias[j], n_batch=bp, qb=256)
            o_s = _sb_sample(qkv_s, cache_sb_kv, j, page_table, sb_bias[j], n_batch=bs, n_new=ts)
            xp = _res_matmul(o_p, sb_w_o, j, zero_d, xp)
            xs = _res_matmul(o_s, sb_w_o, j, zero_d, xs)
            kv_blocks = 2 * H_SB // SUBLANES
            st = _heads_to_rows(qkv_p, n_batch=bp, length=tp, first_block=kv_blocks // 2, block_stride=1, n_blocks=kv_blocks)
            sb_p.append(st.reshape(bp, tp, 2, H_SB, HD))
            sb_s.append(qkv_s[:, hw:].reshape(bs, ts, 2, H_SB, HD))
            i_sb += 1
        xp = _ffn(xp, ln_ffn[i], ffn_w_gate, ffn_w_up, ffn_w_down, i)
        xs = _ffn(xs, ln_ffn[i], ffn_w_gate, ffn_w_up, ffn_w_down, i)
    return (xp.reshape(bp, tp, d), xs.reshape(bs, ts, d), jnp.stack(conv_p), jnp.stack(conv_s),
            jnp.stack(win_p[0]), jnp.stack(win_s[0]), jnp.stack(win_p[1]), jnp.stack(win_s[1]),
            jnp.stack(win_p[2]), jnp.stack(win_s[2]), jnp.stack(sb_p), jnp.stack(sb_s))
```

```python
import functools

import jax
import jax.numpy as jnp
from jax import lax
from jax.experimental import pallas as pl
from jax.experimental.pallas import tpu as pltpu

F32 = jnp.float32
BF16 = jnp.bfloat16

D_MODEL = 2048
CONV_WIDTH = 31
CONV_HIST = CONV_WIDTH - 1
CONV_HALO = 32
SUBLANES = 8
SB_BLOCK_HEADS = 8
SB_Q_SLOTS = 8
DIL_GROUPS = ((128, 1), (512, 4), (2048, 16))
H_DIL = 8
HD = 128
H_SB = 16
PAGE_SIZE = 128
EPS = 1e-6
NEG_INF = -1e30
DIL_SCALE = HD ** -0.5
SB_SCALE = HD ** -0.5

VMEM_LIMIT_BYTES = 56 * 1024 * 1024
QKV_TN = 1024
RES_TM = (2048, 1024)
RES_A_BYTES = 24 * 1024 * 1024


def _cparams(sem):
    return pltpu.CompilerParams(dimension_semantics=sem, vmem_limit_bytes=VMEM_LIMIT_BYTES)


def _log_sigmoid(z):
    return jnp.minimum(z, 0.0) - jnp.log1p(jnp.exp(-jnp.abs(z)))


def _rmsnorm_rows(x_ref, g_ref, xn_ref, r0, rows):
    x = x_ref[r0:r0 + rows, :]
    ms = jnp.mean(x * x, axis=-1, keepdims=True)
    xn = (x * lax.rsqrt(ms + EPS) * g_ref[...]).astype(BF16)
    xn_ref[r0:r0 + rows, :] = xn
    return xn


def _first_column_split(body):
    j = pl.program_id(1)
    pl.when(j == 0)(lambda: body(True))
    pl.when(j != 0)(lambda: body(False))


def _norm_dual_kernel(x_ref, g_ref, w0_ref, w1_ref, b0_ref, b1_ref, o_ref, xn_ref, *, act, row_chunk):
    def body(first):
        w0 = w0_ref[...].astype(BF16)
        w1 = w1_ref[...].astype(BF16)
        for r0 in range(0, o_ref.shape[0], row_chunk):
            a = _rmsnorm_rows(x_ref, g_ref, xn_ref, r0, row_chunk) if first else xn_ref[r0:r0 + row_chunk, :]
            u0 = jnp.dot(a, w0, preferred_element_type=F32) + b0_ref[...]
            u1 = jnp.dot(a, w1, preferred_element_type=F32) + b1_ref[...]
            if act == "swiglu":
                o_ref[r0:r0 + row_chunk, :] = (u0 * jax.nn.sigmoid(u0) * u1).astype(o_ref.dtype)
            else:
                o_ref[r0:r0 + row_chunk, :] = (u0 * jax.nn.sigmoid(u1)).astype(o_ref.dtype)

    _first_column_split(body)


def _layer_weight_spec(k, tn, layer, col_block=0):
    return pl.BlockSpec((None, k, tn), lambda i, j: (layer, 0, j + col_block))


def _norm_dual(x, gain, w0, w1, layer, b0, b1, *, act, col0, col1, n_out, out_dtype, tm, tn):
    m, k = x.shape
    o0, o1 = col0 // tn, col1 // tn
    return pl.pallas_call(
        functools.partial(_norm_dual_kernel, act=act, row_chunk=min(tm, 256)),
        grid=(m // tm, n_out // tn),
        in_specs=[
            pl.BlockSpec((tm, k), lambda i, j: (i, 0)),
            pl.BlockSpec((1, k), lambda i, j: (0, 0)),
            _layer_weight_spec(k, tn, layer, o0),
            _layer_weight_spec(k, tn, layer, o1),
            pl.BlockSpec((1, tn), lambda i, j: (0, j + o0)),
            pl.BlockSpec((1, tn), lambda i, j: (0, j + o1)),
        ],
        out_specs=pl.BlockSpec((tm, tn), lambda i, j: (i, j)),
        out_shape=jax.ShapeDtypeStruct((m, n_out), out_dtype),
        scratch_shapes=[pltpu.VMEM((tm, k), BF16)],
        compiler_params=_cparams(("parallel", "arbitrary")),
        name=f"norm_dual_{act}",
    )(x, gain.reshape(1, k), w0, w1, b0, b1)


def _norm_single_kernel(x_ref, g_ref, w_ref, hg_ref, o_ref, xn_ref, *, n_normed_tiles, row_chunk):
    j = pl.program_id(1)
    tm, tn = o_ref.shape

    def body(first):
        wb = w_ref[...].astype(BF16)
        for r0 in range(0, tm, row_chunk):
            a = _rmsnorm_rows(x_ref, g_ref, xn_ref, r0, row_chunk) if first else xn_ref[r0:r0 + row_chunk, :]
            acc = jnp.dot(a, wb, preferred_element_type=F32)
            if n_normed_tiles:
                hg = hg_ref[0]
                normed = []
                for c in range(tn // HD):
                    blk = acc[:, c * HD:(c + 1) * HD]
                    ms = jnp.mean(blk * blk, axis=-1, keepdims=True)
                    normed.append(blk * lax.rsqrt(ms + EPS) * hg)
                acc = jnp.where(j < n_normed_tiles, jnp.concatenate(normed, axis=1), acc)
            o_ref[r0:r0 + row_chunk, :] = acc

    _first_column_split(body)


def _norm_single(x, gain, w, layer, head_gains, *, n_normed_cols, tm, tn):
    m, k = x.shape
    n = w.shape[2]
    n_normed_tiles = n_normed_cols // tn
    n_sec = head_gains.shape[0]
    tiles_per_sec = max(n_normed_tiles // n_sec, 1)
    return pl.pallas_call(
        functools.partial(_norm_single_kernel, n_normed_tiles=n_normed_tiles, row_chunk=min(tm, 256)),
        grid=(m // tm, n // tn),
        in_specs=[
            pl.BlockSpec((tm, k), lambda i, j: (i, 0)),
            pl.BlockSpec((1, k), lambda i, j: (0, 0)),
            _layer_weight_spec(k, tn, layer),
            pl.BlockSpec((1, 1, HD), lambda i, j: (jnp.minimum(j // tiles_per_sec, n_sec - 1), 0, 0)),
        ],
        out_specs=pl.BlockSpec((tm, tn), lambda i, j: (i, j)),
        out_shape=jax.ShapeDtypeStruct((m, n), F32),
        scratch_shapes=[pltpu.VMEM((tm, k), BF16)],
        compiler_params=_cparams(("parallel", "arbitrary")),
        name="norm_single",
    )(x, gain.reshape(1, k), w, head_gains.reshape(n_sec, 1, HD))


def _res_kernel(a_ref, w_ref, b_ref, r_ref, o_ref):
    acc = jnp.dot(a_ref[...], w_ref[...].astype(BF16), preferred_element_type=F32)
    o_ref[...] = r_ref[...] + (acc + b_ref[...])


def _res_matmul(a, w, layer, bias, res):
    m, k = a.shape
    n = w.shape[2]
    tm = next((t for t in RES_TM if m % t == 0 and 2 * t * k * a.dtype.itemsize <= RES_A_BYTES), m)
    tn = 512 if k * w.dtype.itemsize <= 2048 * 4 * 2 else 256
    return pl.pallas_call(
        _res_kernel,
        grid=(m // tm, n // tn),
        in_specs=[
            pl.BlockSpec((tm, k), lambda i, j: (i, 0)),
            _layer_weight_spec(k, tn, layer),
            pl.BlockSpec((1, tn), lambda i, j: (0, j)),
            pl.BlockSpec((tm, tn), lambda i, j: (i, j)),
        ],
        out_specs=pl.BlockSpec((tm, tn), lambda i, j: (i, j)),
        out_shape=jax.ShapeDtypeStruct((m, n), F32),
        compiler_params=_cparams(("parallel", "arbitrary")),
        name="res_matmul",
    )(a, w, bias.reshape(1, n), res)


def _conv_kernel(hist_ref, prev_ref, cur_ref, w_ref, bdw_ref, lg_ref, lb_ref, o_ref, win_ref, c_ref,
                 *, tt, rows, cols):
    d = cur_ref.shape[1]
    first = pl.program_id(1) == 0
    win_ref[0:CONV_HALO, :] = jnp.where(first, hist_ref[0], prev_ref[...])
    win_ref[CONV_HALO:CONV_HALO + tt, :] = cur_ref[...]
    off = CONV_HALO - CONV_HIST
    for c0 in range(0, d, cols):
        for r0 in range(0, tt, rows):
            acc = jnp.zeros((rows, cols), F32)
            for s in range(SUBLANES):
                ext = rows + (SUBLANES if s else 0)
                part = jnp.zeros((ext // SUBLANES, SUBLANES, cols), F32)
                for k in range(CONV_WIDTH):
                    if (off + k) % SUBLANES != s:
                        continue
                    a = r0 + off + k - s
                    tap = win_ref[a:a + ext, c0:c0 + cols].reshape(ext // SUBLANES, SUBLANES, cols)
                    part = part + tap * w_ref[k * SUBLANES:(k + 1) * SUBLANES, c0:c0 + cols][None]
                acc = acc + part.reshape(ext, cols)[s:s + rows]
            c_ref[r0:r0 + rows, c0:c0 + cols] = acc + bdw_ref[:, c0:c0 + cols]
    c = c_ref[...]
    xc = c - jnp.mean(c, axis=-1, keepdims=True)
    y = xc * lax.rsqrt(jnp.mean(xc * xc, axis=-1, keepdims=True) + EPS) * lg_ref[...] + lb_ref[...]
    o_ref[...] = (y * jax.nn.sigmoid(y)).astype(o_ref.dtype)


def _conv_core(g, hist, w_dw, b_dw, ln_g, ln_b, *, n_batch, tt):
    m, d = g.shape
    t = m // n_batch
    nt = t // tt
    ratio = tt // CONV_HALO if tt >= CONV_HALO else None
    if ratio is None:
        prev_spec = pl.BlockSpec((CONV_HALO, d), lambda b, i: (0, 0))
        prev = hist.reshape(n_batch * CONV_HALO, d)
    else:
        prev_spec = pl.BlockSpec((CONV_HALO, d), lambda b, i: (jnp.maximum((b * nt + i) * ratio - 1, 0), 0))
        prev = g
    w_rep = jnp.repeat(w_dw, SUBLANES, axis=0)
    return pl.pallas_call(
        functools.partial(_conv_kernel, tt=tt, rows=min(tt, 128), cols=HD),
        grid=(n_batch, nt),
        in_specs=[
            pl.BlockSpec((1, CONV_HALO, d), lambda b, i: (b, 0, 0)),
            prev_spec,
            pl.BlockSpec((tt, d), lambda b, i: (b * nt + i, 0)),
            pl.BlockSpec((CONV_WIDTH * SUBLANES, d), lambda b, i: (0, 0)),
            pl.BlockSpec((1, d), lambda b, i: (0, 0)),
            pl.BlockSpec((1, d), lambda b, i: (0, 0)),
            pl.BlockSpec((1, d), lambda b, i: (0, 0)),
        ],
        out_specs=pl.BlockSpec((tt, d), lambda b, i: (b * nt + i, 0)),
        out_shape=jax.ShapeDtypeStruct((m, d), BF16),
        scratch_shapes=[pltpu.VMEM((CONV_HALO + tt, d), F32), pltpu.VMEM((tt, d), F32)],
        compiler_params=_cparams(("parallel", "arbitrary")),
        name="conv_core",
    )(hist, prev, g, w_rep, b_dw.reshape(1, d), ln_g.reshape(1, d), ln_b.reshape(1, d))


def _softmax_block(s, mask):
    s = jnp.where(mask, s, NEG_INF)
    m = jnp.max(s, axis=-1, keepdims=True)
    p = jnp.exp(s - m)
    den = jnp.sum(p, axis=-1, keepdims=True)
    return p, den, m + jnp.log(den)


def _mix_groups(outs, lses):
    m = jnp.maximum(jnp.maximum(lses[0], lses[1]), lses[2])
    es = [jnp.exp(l - m) for l in lses]
    tot = es[0] + es[1] + es[2]
    return (es[0] * outs[0] + es[1] * outs[1] + es[2] * outs[2]) / tot


DIL_UNROLL = 8


def _largest_divisor(n, cap):
    return max(u for u in range(1, cap + 1) if n % u == 0)


def _dil_prompt_kernel(*refs, t, qb, heads):
    n_in = 9 * heads
    in_refs, (o_ref, og_ref, lse_ref) = refs[:n_in], refs[n_in:]
    n_g = len(DIL_GROUPS)
    row = lax.broadcasted_iota(jnp.int32, (qb, 2 * qb), 0)
    col = lax.broadcasted_iota(jnp.int32, (qb, 2 * qb), 1)
    band = (col >= row) & (col <= row + qb)
    causal = (lax.broadcasted_iota(jnp.int32, (qb, qb), 1)
              <= lax.broadcasted_iota(jnp.int32, (qb, qb), 0))

    def rows(start, size, d):
        return pl.ds(start, size) if d == 1 else pl.ds(start, size, stride=d)

    for g, (window, d) in enumerate(DIL_GROUPS):
        assert window == qb * d, "a query block plus the previous one must cover the window"
        nblk = t // (d * qb)

        def attend(q_start, k_start, nk, mask, g=g, d=d):
            scores, values = [], []
            for hh in range(heads):
                q_ref, k_ref, v_ref = (in_refs[hh * 9 + sec * n_g + g] for sec in range(3))
                qv = q_ref[0, rows(q_start, qb, d), :].astype(BF16)
                kv = k_ref[0, rows(k_start, nk, d), :].astype(BF16)
                values.append(v_ref[0, rows(k_start, nk, d), :].astype(BF16))
                scores.append(lax.dot_general(qv, kv, (((1,), (1,)), ((), ())),
                                              preferred_element_type=F32) * DIL_SCALE)
            for hh in range(heads):
                p, den, lse = _softmax_block(scores[hh], mask)
                o = jnp.dot(p.astype(BF16), values[hh], preferred_element_type=F32) / den
                og_ref[hh * n_g + g, rows(q_start, qb, d), :] = o
                lse_ref[hh * n_g + g, rows(q_start, qb, d), :] = jnp.broadcast_to(lse, (qb, HD))

        def residue(r, carry, d=d, nblk=nblk, attend=attend):
            attend(r, r, qb, causal)

            def later(i, c):
                attend(r + i * qb * d, r + (i - 1) * qb * d, 2 * qb, band)
                return c

            if nblk > 1:
                lax.fori_loop(1, nblk, later, 0, unroll=_largest_divisor(nblk - 1, DIL_UNROLL))
            return carry

        lax.fori_loop(0, d, residue, 0, unroll=_largest_divisor(d, DIL_UNROLL) if nblk == 1 else 1)

    for hh in range(heads):
        outs = [og_ref[hh * n_g + g] for g in range(n_g)]
        lses = [lse_ref[hh * n_g + g] for g in range(n_g)]
        o_ref[0, :, hh * HD:(hh + 1) * HD] = _mix_groups(outs, lses).astype(o_ref.dtype)


def _dil_prompt(qkv, *, n_batch, heads=2):
    m, n = qkv.shape
    t = m // n_batch
    qkv3 = qkv.reshape(n_batch, t, n)
    n_g = len(DIL_GROUPS)
    nh = n_g * H_DIL

    def spec(hh, sec, g):
        return pl.BlockSpec((1, t, HD), lambda b, h: (b, 0, sec * nh + g * H_DIL + h * heads + hh))

    out = pl.pallas_call(
        functools.partial(_dil_prompt_kernel, t=t, qb=128, heads=heads),
        grid=(n_batch, H_DIL // heads),
        in_specs=[spec(hh, sec, g) for hh in range(heads) for sec in range(3) for g in range(n_g)],
        out_specs=pl.BlockSpec((1, t, heads * HD), lambda b, h: (b, 0, h)),
        out_shape=jax.ShapeDtypeStruct((n_batch, t, H_DIL * HD), BF16),
        scratch_shapes=[pltpu.VMEM((heads * n_g, t, HD), F32), pltpu.VMEM((heads * n_g, t, HD), F32)],
        compiler_params=_cparams(("parallel", "parallel")),
        name="dil_prompt",
    )(*([qkv3] * (9 * heads)))
    return out.reshape(m, H_DIL * HD)


def _dil_sample_kernel(qkv_ref, ck0, ck1, ck2, cv0, cv1, cv2, o_ref, *, n_new):
    ck_refs, cv_refs = (ck0, ck1, ck2), (cv0, cv1, cv2)
    nq = qkv_ref.shape[1]
    nh = len(DIL_GROUPS) * H_DIL
    n_col = lax.broadcasted_iota(jnp.int32, (nq, 1), 0)
    outs = [[None] * len(DIL_GROUPS) for _ in range(H_DIL)]
    lses = [[None] * len(DIL_GROUPS) for _ in range(H_DIL)]
    for g, (window, d) in enumerate(DIL_GROUPS):
        length = ck_refs[g].shape[1]
        assert length == window and length % d == 0
        k_flat = ck_refs[g].reshape(length * H_DIL, HD)
        v_flat = cv_refs[g].reshape(length * H_DIL, HD)
        n_i = lax.broadcasted_iota(jnp.int32, (nq, length), 0)
        c_i = lax.broadcasted_iota(jnp.int32, (nq, length), 1)
        mask = (c_i >= n_i) & (((c_i - n_i) & (d - 1)) == 0)
        for h in range(H_DIL):
            def cols(sec, g=g, h=h):
                c0 = (sec * nh + g * H_DIL + h) * HD
                return slice(c0, c0 + HD)

            qf = qkv_ref[0, :, cols(0)]
            kc = k_flat[pl.ds(h, length, stride=H_DIL), :].astype(BF16)
            vc = v_flat[pl.ds(h, length, stride=H_DIL), :].astype(BF16)
            s = lax.dot_general(qf.astype(BF16), kc, (((1,), (1,)), ((), ())), preferred_element_type=F32) * DIL_SCALE
            s = jnp.where(mask, s, NEG_INF)
            m = jnp.max(s, axis=-1, keepdims=True)
            new_scores = []
            for j in range(n_new):
                sj = jnp.sum(qf * qkv_ref[0, j:j + 1, cols(1)], axis=-1, keepdims=True) * DIL_SCALE
                sj = jnp.where((n_col >= j) & (((n_col - j) & (d - 1)) == 0), sj, NEG_INF)
                new_scores.append(sj)
                m = jnp.maximum(m, sj)
            p = jnp.exp(s - m)
            den = jnp.sum(p, axis=-1, keepdims=True)
            o = jnp.dot(p.astype(BF16), vc, preferred_element_type=F32)
            for j in range(n_new):
                pj = jnp.exp(new_scores[j] - m)
                den = den + pj
                o = o + pj * qkv_ref[0, j:j + 1, cols(2)]
            outs[h][g] = o / den
            lses[h][g] = m + jnp.log(den)
    for h in range(H_DIL):
        o_ref[0, :, h * HD:(h + 1) * HD] = _mix_groups(outs[h], lses[h]).astype(o_ref.dtype)


def _dil_sample(qkv, caches, layer, *, n_batch, n_new):
    n = qkv.shape[1]
    nq = SUBLANES
    qkv3 = jnp.pad(qkv.reshape(n_batch, n_new, n), ((0, 0), (0, nq - n_new), (0, 0)))
    flat = [c.reshape(c.shape[0] * n_batch, c.shape[2], 2 * H_DIL, HD) for c in caches]
    ck_specs = [pl.BlockSpec((1, c.shape[1], H_DIL, HD), lambda b: (layer * n_batch + b, 0, 0, 0)) for c in flat]
    cv_specs = [pl.BlockSpec((1, c.shape[1], H_DIL, HD), lambda b: (layer * n_batch + b, 0, 1, 0)) for c in flat]
    out = pl.pallas_call(
        functools.partial(_dil_sample_kernel, n_new=n_new),
        grid=(n_batch,),
        in_specs=[pl.BlockSpec((1, nq, n), lambda b: (b, 0, 0))] + ck_specs + cv_specs,
        out_specs=pl.BlockSpec((1, nq, H_DIL * HD), lambda b: (b, 0, 0)),
        out_shape=jax.ShapeDtypeStruct((n_batch, nq, H_DIL * HD), BF16),
        compiler_params=_cparams(("parallel",)),
        name="dil_sample",
    )(qkv3, *flat, *flat)
    return out[:, :n_new].reshape(n_batch * n_new, H_DIL * HD)


def _split_bf16(x):
    hi = x.astype(BF16)
    lo = (x - hi.astype(F32)).astype(BF16)
    return hi, lo


def _sb_prompt_kernel(bias_ref, q_ref, k_ref, v_ref, o_ref, acc_ref, surv_ref, *, qb, heads):
    hp = pl.program_id(1)
    i = pl.program_id(2)
    row = lax.broadcasted_iota(jnp.int32, (qb, qb), 0)
    col = lax.broadcasted_iota(jnp.int32, (qb, qb), 1)
    later = (row > col).astype(BF16)
    later2 = jnp.concatenate([later, later], axis=0)
    strict = col < row
    lanes = [slice(hh * HD, (hh + 1) * HD) for hh in range(heads)]
    biases = [bias_ref[0, hp * heads + hh] for hh in range(heads)]
    qs = [(q_ref[0, :, lanes[hh]] * SB_SCALE).astype(BF16) for hh in range(heads)]

    def blocks(k_start, mask):
        zs = []
        for hh in range(heads):
            kv = k_ref[0, pl.ds(k_start, qb), lanes[hh]].astype(BF16)
            zs.append(lax.dot_general(qs[hh], kv, (((1,), (1,)), ((), ())), preferred_element_type=F32) + biases[hh])
        lss, lnegs, betweens = [], [], []
        for hh in range(heads):
            z = zs[hh]
            t = jnp.log(1.0 + jnp.exp(-jnp.abs(z)))
            ls = jnp.minimum(z, 0.0) - t
            lneg = ls - z
            if mask is not None:
                lneg = jnp.where(mask, lneg, 0.0)
            hi, lo = _split_bf16(lneg)
            betweens.append(jnp.dot(jnp.concatenate([hi, lo], axis=1), later2, preferred_element_type=F32))
            lss.append(ls)
            lnegs.append(lneg)
        for hh in range(heads):
            surv = surv_ref[hh]
            a = jnp.exp(lss[hh] + betweens[hh] + surv)
            if mask is not None:
                a = jnp.where(mask, a, 0.0)
            vv = v_ref[0, pl.ds(k_start, qb), lanes[hh]].astype(BF16)
            acc_ref[hh] += jnp.dot(a.astype(BF16), vv, preferred_element_type=F32)
            surv_ref[hh] = surv + jnp.sum(lnegs[hh], axis=-1, keepdims=True)

    acc_ref[...] = jnp.zeros(acc_ref.shape, F32)
    surv_ref[...] = jnp.zeros(surv_ref.shape, F32)
    blocks(pl.multiple_of(i * qb, qb), strict)

    def older(step, carry):
        blocks(pl.multiple_of((i - 1 - step) * qb, qb), None)
        return carry

    lax.fori_loop(0, i, older, 0)
    for hh in range(heads):
        o_ref[0, :, lanes[hh]] = acc_ref[hh].astype(o_ref.dtype)


def _sb_prompt(qkv, bias, *, n_batch, qb, heads=8):
    m, n = qkv.shape
    t = m // n_batch
    qkv3 = qkv.reshape(n_batch, t, n)
    hg = H_SB // heads
    w = heads * HD
    out = pl.pallas_call(
        functools.partial(_sb_prompt_kernel, qb=qb, heads=heads),
        grid=(n_batch, hg, t // qb),
        in_specs=[
            pl.BlockSpec(memory_space=pltpu.SMEM),
            pl.BlockSpec((1, qb, w), lambda b, h, i: (b, i, h)),
            pl.BlockSpec((1, t, w), lambda b, h, i: (b, 0, hg + h)),
            pl.BlockSpec((1, t, w), lambda b, h, i: (b, 0, 2 * hg + h)),
        ],
        out_specs=pl.BlockSpec((1, qb, w), lambda b, h, i: (b, i, h)),
        out_shape=jax.ShapeDtypeStruct((n_batch, t, H_SB * HD), BF16),
        scratch_shapes=[pltpu.VMEM((heads, qb, HD), F32), pltpu.VMEM((heads, qb, 1), F32)],
        compiler_params=_cparams(("parallel", "parallel", "arbitrary")),
        name="sb_prompt",
    )(bias.reshape(1, H_SB), qkv3, qkv3, qkv3)
    return out.reshape(m, H_SB * HD)


def _sb_sample_kernel(pt_ref, bias_ref, q_ref, *rest, n_new, pages_per_step):
    blocks_per_page = 2 * H_SB // SB_BLOCK_HEADS
    new_refs = rest[:blocks_per_page]
    page_refs = rest[blocks_per_page:blocks_per_page * (1 + pages_per_step)]
    o_ref, acc_ref, surv_ref = rest[blocks_per_page * (1 + pages_per_step):]
    p = pl.program_id(1)
    row = lax.broadcasted_iota(jnp.int32, (PAGE_SIZE, PAGE_SIZE), 0)
    col = lax.broadcasted_iota(jnp.int32, (PAGE_SIZE, PAGE_SIZE), 1)
    later = (row > col).astype(BF16)
    later2 = jnp.concatenate([later, later], axis=0)
    qs = [(q_ref[0, h] * SB_SCALE).astype(BF16) for h in range(H_SB)]

    def head_tile(ref, i):
        flat = ref.reshape(PAGE_SIZE * SB_BLOCK_HEADS, HD)
        return flat[pl.ds(i, PAGE_SIZE, stride=SB_BLOCK_HEADS), :].astype(BF16)

    def page(blocks, mask, surv):
        zs = []
        for h in range(H_SB):
            kh = head_tile(blocks[h // SB_BLOCK_HEADS], h % SB_BLOCK_HEADS)
            z = lax.dot_general(qs[h], kh, (((1,), (1,)), ((), ())), preferred_element_type=F32)
            zs.append(z + bias_ref[0, h])
        z = jnp.concatenate(zs, axis=0)
        t = jnp.log(1.0 + jnp.exp(-jnp.abs(z)))
        ls = jnp.minimum(z, 0.0) - t
        lneg = ls - z
        if mask is not None:
            lneg = jnp.where(mask, lneg, 0.0)
        hi, lo = _split_bf16(lneg)
        between = jnp.dot(jnp.concatenate([hi, lo], axis=1), later2, preferred_element_type=F32)
        a = jnp.exp(ls + between + surv)
        if mask is not None:
            a = jnp.where(mask, a, 0.0)
        outs = []
        v0 = H_SB // SB_BLOCK_HEADS
        for h in range(H_SB):
            vh = head_tile(blocks[v0 + h // SB_BLOCK_HEADS], h % SB_BLOCK_HEADS)
            ah = a[h * SB_Q_SLOTS:(h + 1) * SB_Q_SLOTS].astype(BF16)
            outs.append(jnp.dot(ah, vh, preferred_element_type=F32))
        return jnp.concatenate(outs, axis=0), surv + jnp.sum(lneg, axis=-1, keepdims=True)

    @pl.when(p == 0)
    def _():
        mask = (col < (row & (SB_Q_SLOTS - 1))) & (col < n_new)
        contrib, surv = page(new_refs, mask, jnp.zeros((H_SB * SB_Q_SLOTS, 1), F32))
        acc_ref[...] = contrib
        surv_ref[...] = surv

    surv = surv_ref[...]
    total = acc_ref[...]
    for j in range(pages_per_step):
        contrib, surv = page(page_refs[j * blocks_per_page:(j + 1) * blocks_per_page], None, surv)
        total = total + contrib
    acc_ref[...] = total
    surv_ref[...] = surv

    @pl.when(p == pl.num_programs(1) - 1)
    def _():
        for h in range(H_SB):
            o_ref[0, :, h * HD:(h + 1) * HD] = acc_ref[h * SB_Q_SLOTS:(h + 1) * SB_Q_SLOTS, :].astype(o_ref.dtype)


def _sb_sample(qkv, pools, layer, page_table, bias, *, n_batch, n_new, pages_per_step=8):
    hw = H_SB * HD
    n_pool = pools.shape[1]
    n_pages = page_table.shape[1]
    blocks_per_page = 2 * H_SB // SB_BLOCK_HEADS
    q = qkv[:, :hw].reshape(n_batch, n_new, H_SB, HD).transpose(0, 2, 1, 3)
    q = jnp.pad(q, ((0, 0), (0, 0), (0, SB_Q_SLOTS - n_new), (0, 0)))
    new_kv = jnp.pad(qkv[:, hw:].reshape(n_batch, n_new, 2 * H_SB, HD),
                     ((0, 0), (0, PAGE_SIZE - n_new), (0, 0), (0, 0)))
    pool4 = pools.reshape(pools.shape[0] * n_pool, PAGE_SIZE, 2 * H_SB, HD)
    blk = (1, PAGE_SIZE, SB_BLOCK_HEADS, HD)

    def page_spec(j, hb):
        return pl.BlockSpec(
            blk, lambda b, p, pt: (layer * n_pool + pt[b, n_pages - 1 - (p * pages_per_step + j)], 0, hb, 0))

    grid_spec = pltpu.PrefetchScalarGridSpec(
        num_scalar_prefetch=1,
        grid=(n_batch, n_pages // pages_per_step),
        in_specs=[pl.BlockSpec(memory_space=pltpu.SMEM),
                  pl.BlockSpec((1, H_SB, SB_Q_SLOTS, HD), lambda b, p, pt: (b, 0, 0, 0))]
        + [pl.BlockSpec(blk, lambda b, p, pt, hb=hb: (b, 0, hb, 0)) for hb in range(blocks_per_page)]
        + [page_spec(j, hb) for j in range(pages_per_step) for hb in range(blocks_per_page)],
        out_specs=pl.BlockSpec((1, SB_Q_SLOTS, hw), lambda b, p, pt: (b, 0, 0)),
        scratch_shapes=[pltpu.VMEM((H_SB * SB_Q_SLOTS, HD), F32), pltpu.VMEM((H_SB * SB_Q_SLOTS, 1), F32)],
    )
    out = pl.pallas_call(
        functools.partial(_sb_sample_kernel, n_new=n_new, pages_per_step=pages_per_step),
        grid_spec=grid_spec,
        out_shape=jax.ShapeDtypeStruct((n_batch, SB_Q_SLOTS, hw), BF16),
        compiler_params=_cparams(("parallel", "arbitrary")),
        name="sb_sample",
    )(page_table, bias.reshape(1, H_SB), q, *([new_kv] * blocks_per_page),
      *([pool4] * (pages_per_step * blocks_per_page)))
    return out[:, :n_new].reshape(n_batch * n_new, hw)


def _heads_to_rows_kernel(x_ref, o_ref):
    tm = x_ref.shape[0]
    flat = o_ref.reshape(tm * SUBLANES, HD)
    for c in range(SUBLANES):
        flat[pl.ds(c, tm, stride=SUBLANES), :] = x_ref[:, c * HD:(c + 1) * HD]


def _heads_to_rows(x, *, n_batch, length, first_block, block_stride, n_blocks):
    m, _ = x.shape
    t = m // n_batch
    tm = min(length, 512)
    nt = length // tm
    first = (t - length) // tm
    w = SUBLANES * HD
    return pl.pallas_call(
        _heads_to_rows_kernel,
        grid=(n_batch, nt, n_blocks),
        in_specs=[pl.BlockSpec((tm, w), lambda b, i, j: (b * (t // tm) + first + i, first_block + j * block_stride))],
        out_specs=pl.BlockSpec((tm, SUBLANES, HD), lambda b, i, j: (b * nt + i, j, 0)),
        out_shape=jax.ShapeDtypeStruct((n_batch * length, n_blocks * SUBLANES, HD), F32),
        compiler_params=_cparams(("parallel", "parallel", "parallel")),
        name="heads_to_rows",
    )(x)


def _tiles(m):
    return 1024 if m % 1024 == 0 else m


def _to_bf16_kernel(w_ref, o_ref):
    o_ref[...] = w_ref[...].astype(BF16)


def _to_bf16(w, *, rows):
    n_layers, k, n = w.shape
    return pl.pallas_call(
        _to_bf16_kernel,
        grid=(n_layers, k // rows),
        in_specs=[pl.BlockSpec((None, rows, n), lambda l, i: (l, i, 0))],
        out_specs=pl.BlockSpec((None, rows, n), lambda l, i: (l, i, 0)),
        out_shape=jax.ShapeDtypeStruct(w.shape, BF16),
        compiler_params=_cparams(("parallel", "parallel")),
        name="to_bf16",
    )(w)


def _ffn(x, gain, w_gate, w_up, w_down, layer):
    m = x.shape[0]
    d_ff = w_gate.shape[2]
    zeros = jnp.zeros((1, d_ff), F32)
    a = _norm_dual(x, gain, w_gate, w_up, layer, zeros, zeros, act="swiglu", col0=0, col1=0, n_out=d_ff,
                   out_dtype=BF16, tm=_tiles(m), tn=512)
    return _res_matmul(a, w_down, layer, jnp.zeros((x.shape[1],), F32), x)


def _conv_layer(x, gain, hist, w1, b1, w_dw, b_dw, ln_g, ln_b, w2, b2, layer, *, n_batch):
    m, d = x.shape
    t = m // n_batch
    b1r = b1.reshape(1, 2 * d)
    g = _norm_dual(x, gain, w1, w1, layer, b1r, b1r, act="glu", col0=0, col1=d, n_out=d, out_dtype=F32,
                   tm=_tiles(m), tn=512)
    if t >= 256:
        tt, g_pad = 256, g
    else:
        tt = 8
        g_pad = jnp.pad(g.reshape(n_batch, t, d), ((0, 0), (0, tt - t), (0, 0))).reshape(n_batch * tt, d)
    hist_pad = jnp.pad(hist, ((0, 0), (CONV_HALO - CONV_HIST, 0), (0, 0)))
    s = _conv_core(g_pad, hist_pad, w_dw, b_dw, ln_g, ln_b, n_batch=n_batch, tt=tt)
    if tt != 256:
        s = s.reshape(n_batch, tt, d)[:, :t].reshape(m, d)
    x = _res_matmul(s, w2, layer, b2, x)
    state = jnp.concatenate([hist, g.reshape(n_batch, t, d)], axis=1)[:, -CONV_HIST:]
    return x, state


def _dil_qkv(x, gain, w_qkv, layer, q_gain, k_gain):
    m = x.shape[0]
    third = w_qkv.shape[2] // 3
    return _norm_single(x, gain, w_qkv, layer, jnp.stack([q_gain, k_gain]), n_normed_cols=2 * third,
                        tm=_tiles(m), tn=QKV_TN)


def _dil_states(qkv, n_batch):
    t = qkv.shape[0] // n_batch
    q5 = qkv.reshape(n_batch, t, 3, len(DIL_GROUPS), H_DIL, HD)
    return [jnp.stack([q5[:, :, 1, g], q5[:, :, 2, g]], axis=2) for g in range(len(DIL_GROUPS))]


def kernel(x_prompt, x_sample, state_conv, cache_win_g0, cache_win_g1, cache_win_g2, cache_sb_kv, page_table, ln_mix, ln_ffn, conv_w1, conv_b1, conv_w_dw, conv_b_dw, conv_ln_g, conv_ln_b, conv_w2, conv_b2, dil_w_qkv, dil_q_gain, dil_k_gain, dil_w_o, sb_w_qkv, sb_w_o, sb_bias, ffn_w_gate, ffn_w_up, ffn_w_down):
    bp, tp, d = x_prompt.shape
    bs, ts, _ = x_sample.shape
    depth = ln_mix.shape[0]
    xp = x_prompt.reshape(bp * tp, d)
    xs = x_sample.reshape(bs * ts, d)
    win_caches = (cache_win_g0, cache_win_g1, cache_win_g2)
    conv_p, conv_s, sb_p, sb_s = [], [], [], []
    win_p = [[] for _ in DIL_GROUPS]
    win_s = [[] for _ in DIL_GROUPS]
    i_conv = i_dil = i_sb = 0
    zero_d = jnp.zeros((d,), F32)
    w_down = _to_bf16(ffn_w_down, rows=512)
    for i in range(depth):
        kind = i % 3
        if kind == 0:
            j = i_conv
            prm = (conv_w1, conv_b1[j], conv_w_dw[j], conv_b_dw[j], conv_ln_g[j], conv_ln_b[j], conv_w2, conv_b2[j], j)
            xp, st_p = _conv_layer(xp, ln_mix[i], jnp.zeros((bp, CONV_HIST, d), F32), *prm, n_batch=bp)
            xs, st_s = _conv_layer(xs, ln_mix[i], state_conv[j], *prm, n_batch=bs)
            conv_p.append(st_p)
            conv_s.append(st_s)
            i_conv += 1
        elif kind == 1:
            j = i_dil
            qkv_p = _dil_qkv(xp, ln_mix[i], dil_w_qkv, j, dil_q_gain[j], dil_k_gain[j])
            qkv_s = _dil_qkv(xs, ln_mix[i], dil_w_qkv, j, dil_q_gain[j], dil_k_gain[j])
            o_p = _dil_prompt(qkv_p, n_batch=bp)
            o_s = _dil_sample(qkv_s, win_caches, j, n_batch=bs, n_new=ts)
            xp = _res_matmul(o_p, dil_w_o, j, zero_d, xp)
            xs = _res_matmul(o_s, dil_w_o, j, zero_d, xs)
            new_s = _dil_states(qkv_s, bs)
            n_g = len(DIL_GROUPS)
            for g, (window, _) in enumerate(DIL_GROUPS):
                keep = min(window, tp)
                st = _heads_to_rows(qkv_p, n_batch=bp, length=keep, first_block=n_g + g, block_stride=n_g, n_blocks=2)
                win_p[g].append(st.reshape(bp, keep, 2, H_DIL, HD))
                length = win_caches[g].shape[2]
                win_s[g].append(jnp.concatenate([win_caches[g][j], new_s[g]], axis=1)[:, -length:])
            i_dil += 1
        else:
            j = i_sb
            hw = H_SB * HD
            ones_gain = jnp.ones((1, HD), F32)
            qkv_p = _norm_single(xp, ln_mix[i], sb_w_qkv, j, ones_gain, n_normed_cols=0, tm=_tiles(xp.shape[0]), tn=QKV_TN)
            qkv_s = _norm_single(xs, ln_mix[i], sb_w_qkv, j, ones_gain, n_normed_cols=0, tm=_tiles(xs.shape[0]), tn=QKV_TN)
            o_p = _sb_prompt(qkv_p, sb_bias[j], n_batch=bp, qb=256)
            o_s = _sb_sample(qkv_s, cache_sb_kv, j, page_table, sb_bias[j], n_batch=bs, n_new=ts)
            xp = _res_matmul(o_p, sb_w_o, j, zero_d, xp)
            xs = _res_matmul(o_s, sb_w_o, j, zero_d, xs)
            kv_blocks = 2 * H_SB // SUBLANES
            st = _heads_to_rows(qkv_p, n_batch=bp, length=tp, first_block=kv_blocks // 2, block_stride=1, n_blocks=kv_blocks)
            sb_p.append(st.reshape(bp, tp, 2, H_SB, HD))
            sb_s.append(qkv_s[:, hw:].reshape(bs, ts, 2, H_SB, HD))
            i_sb += 1
        xp = _ffn(xp, ln_ffn[i], ffn_w_gate, ffn_w_up, w_down, i)
        xs = _ffn(xs, ln_ffn[i], ffn_w_gate, ffn_w_up, w_down, i)
    return (xp.reshape(bp, tp, d), xs.reshape(bs, ts, d), jnp.stack(conv_p), jnp.stack(conv_s),
            jnp.stack(win_p[0]), jnp.stack(win_s[0]), jnp.stack(win_p[1]), jnp.stack(win_s[1]),
            jnp.stack(win_p[2]), jnp.stack(win_s[2]), jnp.stack(sb_p), jnp.stack(sb_s))
```

```python
import functools

import jax
import jax.numpy as jnp
from jax import lax
from jax.experimental import pallas as pl
from jax.experimental.pallas import tpu as pltpu

F32 = jnp.float32
BF16 = jnp.bfloat16

D_MODEL = 2048
CONV_WIDTH = 31
CONV_HIST = CONV_WIDTH - 1
CONV_HALO = 32
SUBLANES = 8
SB_BLOCK_HEADS = 8
SB_Q_SLOTS = 8
DIL_GROUPS = ((128, 1), (512, 4), (2048, 16))
H_DIL = 8
HD = 128
H_SB = 16
PAGE_SIZE = 128
EPS = 1e-6
NEG_INF = -1e30
DIL_SCALE = HD ** -0.5
SB_SCALE = HD ** -0.5

VMEM_LIMIT_BYTES = 56 * 1024 * 1024
QKV_TN = 1024
RES_TM = (2048, 1024)
RES_A_BYTES = 24 * 1024 * 1024


def _cparams(sem):
    return pltpu.CompilerParams(dimension_semantics=sem, vmem_limit_bytes=VMEM_LIMIT_BYTES)


def _log_sigmoid(z):
    return jnp.minimum(z, 0.0) - jnp.log1p(jnp.exp(-jnp.abs(z)))


def _rmsnorm_rows(x_ref, g_ref, xn_ref, r0, rows):
    x = x_ref[r0:r0 + rows, :]
    ms = jnp.mean(x * x, axis=-1, keepdims=True)
    xn = (x * lax.rsqrt(ms + EPS) * g_ref[...]).astype(BF16)
    xn_ref[r0:r0 + rows, :] = xn
    return xn


def _first_column_split(body):
    j = pl.program_id(1)
    pl.when(j == 0)(lambda: body(True))
    pl.when(j != 0)(lambda: body(False))


def _norm_dual_kernel(x_ref, g_ref, w0_ref, w1_ref, b0_ref, b1_ref, o_ref, xn_ref, *, act, row_chunk):
    def body(first):
        w0 = w0_ref[...].astype(BF16)
        w1 = w1_ref[...].astype(BF16)
        for r0 in range(0, o_ref.shape[0], row_chunk):
            a = _rmsnorm_rows(x_ref, g_ref, xn_ref, r0, row_chunk) if first else xn_ref[r0:r0 + row_chunk, :]
            u0 = jnp.dot(a, w0, preferred_element_type=F32) + b0_ref[...]
            u1 = jnp.dot(a, w1, preferred_element_type=F32) + b1_ref[...]
            if act == "swiglu":
                o_ref[r0:r0 + row_chunk, :] = (u0 * jax.nn.sigmoid(u0) * u1).astype(o_ref.dtype)
            else:
                o_ref[r0:r0 + row_chunk, :] = (u0 * jax.nn.sigmoid(u1)).astype(o_ref.dtype)

    _first_column_split(body)


def _layer_weight_spec(k, tn, layer, col_block=0):
    return pl.BlockSpec((None, k, tn), lambda i, j: (layer, 0, j + col_block))


def _norm_dual(x, gain, w0, w1, layer, b0, b1, *, act, col0, col1, n_out, out_dtype, tm, tn):
    m, k = x.shape
    o0, o1 = col0 // tn, col1 // tn
    return pl.pallas_call(
        functools.partial(_norm_dual_kernel, act=act, row_chunk=min(tm, 256)),
        grid=(m // tm, n_out // tn),
        in_specs=[
            pl.BlockSpec((tm, k), lambda i, j: (i, 0)),
            pl.BlockSpec((1, k), lambda i, j: (0, 0)),
            _layer_weight_spec(k, tn, layer, o0),
            _layer_weight_spec(k, tn, layer, o1),
            pl.BlockSpec((1, tn), lambda i, j: (0, j + o0)),
            pl.BlockSpec((1, tn), lambda i, j: (0, j + o1)),
        ],
        out_specs=pl.BlockSpec((tm, tn), lambda i, j: (i, j)),
        out_shape=jax.ShapeDtypeStruct((m, n_out), out_dtype),
        scratch_shapes=[pltpu.VMEM((tm, k), BF16)],
        compiler_params=_cparams(("parallel", "arbitrary")),
        name=f"norm_dual_{act}",
    )(x, gain.reshape(1, k), w0, w1, b0, b1)


def _norm_single_kernel(x_ref, g_ref, w_ref, hg_ref, o_ref, xn_ref, *, n_normed_tiles, row_chunk):
    j = pl.program_id(1)
    tm, tn = o_ref.shape

    def body(first):
        wb = w_ref[...].astype(BF16)
        for r0 in range(0, tm, row_chunk):
            a = _rmsnorm_rows(x_ref, g_ref, xn_ref, r0, row_chunk) if first else xn_ref[r0:r0 + row_chunk, :]
            acc = jnp.dot(a, wb, preferred_element_type=F32)
            if n_normed_tiles:
                hg = hg_ref[0]
                normed = []
                for c in range(tn // HD):
                    blk = acc[:, c * HD:(c + 1) * HD]
                    ms = jnp.mean(blk * blk, axis=-1, keepdims=True)
                    normed.append(blk * lax.rsqrt(ms + EPS) * hg)
                acc = jnp.where(j < n_normed_tiles, jnp.concatenate(normed, axis=1), acc)
            o_ref[r0:r0 + row_chunk, :] = acc

    _first_column_split(body)


def _norm_single(x, gain, w, layer, head_gains, *, n_normed_cols, tm, tn):
    m, k = x.shape
    n = w.shape[2]
    n_normed_tiles = n_normed_cols // tn
    n_sec = head_gains.shape[0]
    tiles_per_sec = max(n_normed_tiles // n_sec, 1)
    return pl.pallas_call(
        functools.partial(_norm_single_kernel, n_normed_tiles=n_normed_tiles, row_chunk=min(tm, 256)),
        grid=(m // tm, n // tn),
        in_specs=[
            pl.BlockSpec((tm, k), lambda i, j: (i, 0)),
            pl.BlockSpec((1, k), lambda i, j: (0, 0)),
            _layer_weight_spec(k, tn, layer),
            pl.BlockSpec((1, 1, HD), lambda i, j: (jnp.minimum(j // tiles_per_sec, n_sec - 1), 0, 0)),
        ],
        out_specs=pl.BlockSpec((tm, tn), lambda i, j: (i, j)),
        out_shape=jax.ShapeDtypeStruct((m, n), F32),
        scratch_shapes=[pltpu.VMEM((tm, k), BF16)],
        compiler_params=_cparams(("parallel", "arbitrary")),
        name="norm_single",
    )(x, gain.reshape(1, k), w, head_gains.reshape(n_sec, 1, HD))


def _res_kernel(a_ref, w_ref, b_ref, r_ref, o_ref):
    acc = jnp.dot(a_ref[...], w_ref[...].astype(BF16), preferred_element_type=F32)
    o_ref[...] = r_ref[...] + (acc + b_ref[...])


def _res_matmul(a, w, layer, bias, res):
    m, k = a.shape
    n = w.shape[2]
    tm = next((t for t in RES_TM if m % t == 0 and 2 * t * k * a.dtype.itemsize <= RES_A_BYTES), m)
    tn = 512 if k * w.dtype.itemsize <= 2048 * 4 * 2 else 256
    return pl.pallas_call(
        _res_kernel,
        grid=(m // tm, n // tn),
        in_specs=[
            pl.BlockSpec((tm, k), lambda i, j: (i, 0)),
            _layer_weight_spec(k, tn, layer),
            pl.BlockSpec((1, tn), lambda i, j: (0, j)),
            pl.BlockSpec((tm, tn), lambda i, j: (i, j)),
        ],
        out_specs=pl.BlockSpec((tm, tn), lambda i, j: (i, j)),
        out_shape=jax.ShapeDtypeStruct((m, n), F32),
        compiler_params=_cparams(("parallel", "arbitrary")),
        name="res_matmul",
    )(a, w, bias.reshape(1, n), res)


def _conv_kernel(hist_ref, prev_ref, cur_ref, w_ref, bdw_ref, lg_ref, lb_ref, o_ref, win_ref, c_ref,
                 *, tt, rows, cols):
    d = cur_ref.shape[1]
    first = pl.program_id(1) == 0
    win_ref[0:CONV_HALO, :] = jnp.where(first, hist_ref[0], prev_ref[...])
    win_ref[CONV_HALO:CONV_HALO + tt, :] = cur_ref[...]
    off = CONV_HALO - CONV_HIST
    for c0 in range(0, d, cols):
        for r0 in range(0, tt, rows):
            acc = jnp.zeros((rows, cols), F32)
            for s in range(SUBLANES):
                ext = rows + (SUBLANES if s else 0)
                part = jnp.zeros((ext // SUBLANES, SUBLANES, cols), F32)
                for k in range(CONV_WIDTH):
                    if (off + k) % SUBLANES != s:
                        continue
                    a = r0 + off + k - s
                    tap = win_ref[a:a + ext, c0:c0 + cols].reshape(ext // SUBLANES, SUBLANES, cols)
                    part = part + tap * w_ref[k * SUBLANES:(k + 1) * SUBLANES, c0:c0 + cols][None]
                acc = acc + part.reshape(ext, cols)[s:s + rows]
            c_ref[r0:r0 + rows, c0:c0 + cols] = acc + bdw_ref[:, c0:c0 + cols]
    c = c_ref[...]
    xc = c - jnp.mean(c, axis=-1, keepdims=True)
    y = xc * lax.rsqrt(jnp.mean(xc * xc, axis=-1, keepdims=True) + EPS) * lg_ref[...] + lb_ref[...]
    o_ref[...] = (y * jax.nn.sigmoid(y)).astype(o_ref.dtype)


def _conv_core(g, hist, w_dw, b_dw, ln_g, ln_b, *, n_batch, tt):
    m, d = g.shape
    t = m // n_batch
    nt = t // tt
    ratio = tt // CONV_HALO if tt >= CONV_HALO else None
    if ratio is None:
        prev_spec = pl.BlockSpec((CONV_HALO, d), lambda b, i: (0, 0))
        prev = hist.reshape(n_batch * CONV_HALO, d)
    else:
        prev_spec = pl.BlockSpec((CONV_HALO, d), lambda b, i: (jnp.maximum((b * nt + i) * ratio - 1, 0), 0))
        prev = g
    w_rep = jnp.repeat(w_dw, SUBLANES, axis=0)
    return pl.pallas_call(
        functools.partial(_conv_kernel, tt=tt, rows=min(tt, 128), cols=HD),
        grid=(n_batch, nt),
        in_specs=[
            pl.BlockSpec((1, CONV_HALO, d), lambda b, i: (b, 0, 0)),
            prev_spec,
            pl.BlockSpec((tt, d), lambda b, i: (b * nt + i, 0)),
            pl.BlockSpec((CONV_WIDTH * SUBLANES, d), lambda b, i: (0, 0)),
            pl.BlockSpec((1, d), lambda b, i: (0, 0)),
            pl.BlockSpec((1, d), lambda b, i: (0, 0)),
            pl.BlockSpec((1, d), lambda b, i: (0, 0)),
        ],
        out_specs=pl.BlockSpec((tt, d), lambda b, i: (b * nt + i, 0)),
        out_shape=jax.ShapeDtypeStruct((m, d), BF16),
        scratch_shapes=[pltpu.VMEM((CONV_HALO + tt, d), F32), pltpu.VMEM((tt, d), F32)],
        compiler_params=_cparams(("parallel", "arbitrary")),
        name="conv_core",
    )(hist, prev, g, w_rep, b_dw.reshape(1, d), ln_g.reshape(1, d), ln_b.reshape(1, d))


def _softmax_block(s, mask):
    s = jnp.where(mask, s, NEG_INF)
    m = jnp.max(s, axis=-1, keepdims=True)
    p = jnp.exp(s - m)
    den = jnp.sum(p, axis=-1, keepdims=True)
    return p, den, m + jnp.log(den)


def _mix_groups(outs, lses):
    m = jnp.maximum(jnp.maximum(lses[0], lses[1]), lses[2])
    es = [jnp.exp(l - m) for l in lses]
    tot = es[0] + es[1] + es[2]
    return (es[0] * outs[0] + es[1] * outs[1] + es[2] * outs[2]) / tot


DIL_UNROLL = 8


def _largest_divisor(n, cap):
    return max(u for u in range(1, cap + 1) if n % u == 0)


def _dil_prompt_kernel(*refs, t, qb, heads):
    n_in = 9 * heads
    in_refs, (o_ref, og_ref, lse_ref) = refs[:n_in], refs[n_in:]
    n_g = len(DIL_GROUPS)
    row = lax.broadcasted_iota(jnp.int32, (qb, 2 * qb), 0)
    col = lax.broadcasted_iota(jnp.int32, (qb, 2 * qb), 1)
    band = (col >= row) & (col <= row + qb)
    causal = (lax.broadcasted_iota(jnp.int32, (qb, qb), 1)
              <= lax.broadcasted_iota(jnp.int32, (qb, qb), 0))

    def rows(start, size, d):
        return pl.ds(start, size) if d == 1 else pl.ds(start, size, stride=d)

    for g, (window, d) in enumerate(DIL_GROUPS):
        assert window == qb * d, "a query block plus the previous one must cover the window"
        nblk = t // (d * qb)

        def attend(q_start, k_start, nk, mask, g=g, d=d):
            scores, values = [], []
            for hh in range(heads):
                q_ref, k_ref, v_ref = (in_refs[hh * 9 + sec * n_g + g] for sec in range(3))
                qv = q_ref[0, rows(q_start, qb, d), :].astype(BF16)
                kv = k_ref[0, rows(k_start, nk, d), :].astype(BF16)
                values.append(v_ref[0, rows(k_start, nk, d), :].astype(BF16))
                scores.append(lax.dot_general(qv, kv, (((1,), (1,)), ((), ())),
                                              preferred_element_type=F32) * DIL_SCALE)
            for hh in range(heads):
                p, den, lse = _softmax_block(scores[hh], mask)
                o = jnp.dot(p.astype(BF16), values[hh], preferred_element_type=F32) / den
                og_ref[hh * n_g + g, rows(q_start, qb, d), :] = o
                lse_ref[hh * n_g + g, rows(q_start, qb, d), :] = jnp.broadcast_to(lse, (qb, HD))

        def residue(r, carry, d=d, nblk=nblk, attend=attend):
            attend(r, r, qb, causal)

            def later(i, c):
                attend(r + i * qb * d, r + (i - 1) * qb * d, 2 * qb, band)
                return c

            if nblk > 1:
                lax.fori_loop(1, nblk, later, 0, unroll=_largest_divisor(nblk - 1, DIL_UNROLL))
            return carry

        lax.fori_loop(0, d, residue, 0, unroll=_largest_divisor(d, DIL_UNROLL) if nblk == 1 else 1)

    for hh in range(heads):
        outs = [og_ref[hh * n_g + g] for g in range(n_g)]
        lses = [lse_ref[hh * n_g + g] for g in range(n_g)]
        o_ref[0, :, hh * HD:(hh + 1) * HD] = _mix_groups(outs, lses).astype(o_ref.dtype)


def _dil_prompt(qkv, *, n_batch, heads=2):
    m, n = qkv.shape
    t = m // n_batch
    qkv3 = qkv.reshape(n_batch, t, n)
    n_g = len(DIL_GROUPS)
    nh = n_g * H_DIL

    def spec(hh, sec, g):
        return pl.BlockSpec((1, t, HD), lambda b, h: (b, 0, sec * nh + g * H_DIL + h * heads + hh))

    out = pl.pallas_call(
        functools.partial(_dil_prompt_kernel, t=t, qb=128, heads=heads),
        grid=(n_batch, H_DIL // heads),
        in_specs=[spec(hh, sec, g) for hh in range(heads) for sec in range(3) for g in range(n_g)],
        out_specs=pl.BlockSpec((1, t, heads * HD), lambda b, h: (b, 0, h)),
        out_shape=jax.ShapeDtypeStruct((n_batch, t, H_DIL * HD), BF16),
        scratch_shapes=[pltpu.VMEM((heads * n_g, t, HD), F32), pltpu.VMEM((heads * n_g, t, HD), F32)],
        compiler_params=_cparams(("parallel", "parallel")),
        name="dil_prompt",
    )(*([qkv3] * (9 * heads)))
    return out.reshape(m, H_DIL * HD)


def _dil_sample_kernel(qkv_ref, ck0, ck1, ck2, cv0, cv1, cv2, o_ref, *, n_new):
    ck_refs, cv_refs = (ck0, ck1, ck2), (cv0, cv1, cv2)
    nq = qkv_ref.shape[1]
    nh = len(DIL_GROUPS) * H_DIL
    n_col = lax.broadcasted_iota(jnp.int32, (nq, 1), 0)
    outs = [[None] * len(DIL_GROUPS) for _ in range(H_DIL)]
    lses = [[None] * len(DIL_GROUPS) for _ in range(H_DIL)]
    for g, (window, d) in enumerate(DIL_GROUPS):
        length = ck_refs[g].shape[1]
        assert length == window and length % d == 0
        k_flat = ck_refs[g].reshape(length * H_DIL, HD)
        v_flat = cv_refs[g].reshape(length * H_DIL, HD)
        n_i = lax.broadcasted_iota(jnp.int32, (nq, length), 0)
        c_i = lax.broadcasted_iota(jnp.int32, (nq, length), 1)
        mask = (c_i >= n_i) & (((c_i - n_i) & (d - 1)) == 0)
        for h in range(H_DIL):
            def cols(sec, g=g, h=h):
                c0 = (sec * nh + g * H_DIL + h) * HD
                return slice(c0, c0 + HD)

            qf = qkv_ref[0, :, cols(0)]
            kc = k_flat[pl.ds(h, length, stride=H_DIL), :].astype(BF16)
            vc = v_flat[pl.ds(h, length, stride=H_DIL), :].astype(BF16)
            s = lax.dot_general(qf.astype(BF16), kc, (((1,), (1,)), ((), ())), preferred_element_type=F32) * DIL_SCALE
            s = jnp.where(mask, s, NEG_INF)
            m = jnp.max(s, axis=-1, keepdims=True)
            new_scores = []
            for j in range(n_new):
                sj = jnp.sum(qf * qkv_ref[0, j:j + 1, cols(1)], axis=-1, keepdims=True) * DIL_SCALE
                sj = jnp.where((n_col >= j) & (((n_col - j) & (d - 1)) == 0), sj, NEG_INF)
                new_scores.append(sj)
                m = jnp.maximum(m, sj)
            p = jnp.exp(s - m)
            den = jnp.sum(p, axis=-1, keepdims=True)
            o = jnp.dot(p.astype(BF16), vc, preferred_element_type=F32)
            for j in range(n_new):
                pj = jnp.exp(new_scores[j] - m)
                den = den + pj
                o = o + pj * qkv_ref[0, j:j + 1, cols(2)]
            outs[h][g] = o / den
            lses[h][g] = m + jnp.log(den)
    for h in range(H_DIL):
        o_ref[0, :, h * HD:(h + 1) * HD] = _mix_groups(outs[h], lses[h]).astype(o_ref.dtype)


def _dil_sample(qkv, caches, layer, *, n_batch, n_new):
    n = qkv.shape[1]
    nq = SUBLANES
    qkv3 = jnp.pad(qkv.reshape(n_batch, n_new, n), ((0, 0), (0, nq - n_new), (0, 0)))
    flat = [c.reshape(c.shape[0] * n_batch, c.shape[2], 2 * H_DIL, HD) for c in caches]
    ck_specs = [pl.BlockSpec((1, c.shape[1], H_DIL, HD), lambda b: (layer * n_batch + b, 0, 0, 0)) for c in flat]
    cv_specs = [pl.BlockSpec((1, c.shape[1], H_DIL, HD), lambda b: (layer * n_batch + b, 0, 1, 0)) for c in flat]
    out = pl.pallas_call(
        functools.partial(_dil_sample_kernel, n_new=n_new),
        grid=(n_batch,),
        in_specs=[pl.BlockSpec((1, nq, n), lambda b: (b, 0, 0))] + ck_specs + cv_specs,
        out_specs=pl.BlockSpec((1, nq, H_DIL * HD), lambda b: (b, 0, 0)),
        out_shape=jax.ShapeDtypeStruct((n_batch, nq, H_DIL * HD), BF16),
        compiler_params=_cparams(("parallel",)),
        name="dil_sample",
    )(qkv3, *flat, *flat)
    return out[:, :n_new].reshape(n_batch * n_new, H_DIL * HD)


def _split_bf16(x):
    hi = x.astype(BF16)
    lo = (x - hi.astype(F32)).astype(BF16)
    return hi, lo


def _sb_prompt_kernel(bias_ref, q_ref, k_ref, v_ref, o_ref, kv_ref, acc_ref, surv_ref, *, qb, heads):
    hp = pl.program_id(1)
    i = pl.program_id(2)
    assert heads == SUBLANES
    kv_flat = kv_ref.reshape(qb * 2 * heads, HD)
    own_rows = pl.ds(pl.multiple_of(i * qb, qb), qb)
    for hh in range(heads):
        head_lanes = slice(hh * HD, (hh + 1) * HD)
        kv_flat[pl.ds(hh, qb, stride=2 * heads), :] = k_ref[0, own_rows, head_lanes]
        kv_flat[pl.ds(heads + hh, qb, stride=2 * heads), :] = v_ref[0, own_rows, head_lanes]
    row = lax.broadcasted_iota(jnp.int32, (qb, qb), 0)
    col = lax.broadcasted_iota(jnp.int32, (qb, qb), 1)
    later = (row > col).astype(BF16)
    later2 = jnp.concatenate([later, later], axis=0)
    strict = col < row
    lanes = [slice(hh * HD, (hh + 1) * HD) for hh in range(heads)]
    biases = [bias_ref[0, hp * heads + hh] for hh in range(heads)]
    qs = [(q_ref[0, :, lanes[hh]] * SB_SCALE).astype(BF16) for hh in range(heads)]

    def blocks(k_start, mask):
        zs = []
        for hh in range(heads):
            kv = k_ref[0, pl.ds(k_start, qb), lanes[hh]].astype(BF16)
            zs.append(lax.dot_general(qs[hh], kv, (((1,), (1,)), ((), ())), preferred_element_type=F32) + biases[hh])
        lss, lnegs, betweens = [], [], []
        for hh in range(heads):
            z = zs[hh]
            t = jnp.log(1.0 + jnp.exp(-jnp.abs(z)))
            ls = jnp.minimum(z, 0.0) - t
            lneg = ls - z
            if mask is not None:
                lneg = jnp.where(mask, lneg, 0.0)
            hi, lo = _split_bf16(lneg)
            betweens.append(jnp.dot(jnp.concatenate([hi, lo], axis=1), later2, preferred_element_type=F32))
            lss.append(ls)
            lnegs.append(lneg)
        for hh in range(heads):
            surv = surv_ref[hh]
            a = jnp.exp(lss[hh] + betweens[hh] + surv)
            if mask is not None:
                a = jnp.where(mask, a, 0.0)
            vv = v_ref[0, pl.ds(k_start, qb), lanes[hh]].astype(BF16)
            acc_ref[hh] += jnp.dot(a.astype(BF16), vv, preferred_element_type=F32)
            surv_ref[hh] = surv + jnp.sum(lnegs[hh], axis=-1, keepdims=True)

    acc_ref[...] = jnp.zeros(acc_ref.shape, F32)
    surv_ref[...] = jnp.zeros(surv_ref.shape, F32)
    blocks(pl.multiple_of(i * qb, qb), strict)

    def older(step, carry):
        blocks(pl.multiple_of((i - 1 - step) * qb, qb), None)
        return carry

    lax.fori_loop(0, i, older, 0)
    for hh in range(heads):
        o_ref[0, :, lanes[hh]] = acc_ref[hh].astype(o_ref.dtype)


def _sb_prompt(qkv, bias, *, n_batch, qb, heads=8):
    m, n = qkv.shape
    t = m // n_batch
    qkv3 = qkv.reshape(n_batch, t, n)
    hg = H_SB // heads
    w = heads * HD
    out, kv = pl.pallas_call(
        functools.partial(_sb_prompt_kernel, qb=qb, heads=heads),
        grid=(n_batch, hg, t // qb),
        in_specs=[
            pl.BlockSpec(memory_space=pltpu.SMEM),
            pl.BlockSpec((1, qb, w), lambda b, h, i: (b, i, h)),
            pl.BlockSpec((1, t, w), lambda b, h, i: (b, 0, hg + h)),
            pl.BlockSpec((1, t, w), lambda b, h, i: (b, 0, 2 * hg + h)),
        ],
        out_specs=[pl.BlockSpec((1, qb, w), lambda b, h, i: (b, i, h)),
                   pl.BlockSpec((1, qb, 2, 1, heads, HD), lambda b, h, i: (b, i, 0, h, 0, 0))],
        out_shape=[jax.ShapeDtypeStruct((n_batch, t, H_SB * HD), BF16),
                   jax.ShapeDtypeStruct((n_batch, t, 2, hg, heads, HD), F32)],
        scratch_shapes=[pltpu.VMEM((heads, qb, HD), F32), pltpu.VMEM((heads, qb, 1), F32)],
        compiler_params=_cparams(("parallel", "parallel", "arbitrary")),
        name="sb_prompt",
    )(bias.reshape(1, H_SB), qkv3, qkv3, qkv3)
    return out.reshape(m, H_SB * HD), kv.reshape(n_batch, t, 2, H_SB, HD)


def _sb_sample_kernel(pt_ref, bias_ref, q_ref, *rest, n_new, pages_per_step):
    blocks_per_page = 2 * H_SB // SB_BLOCK_HEADS
    new_refs = rest[:blocks_per_page]
    page_refs = rest[blocks_per_page:blocks_per_page * (1 + pages_per_step)]
    o_ref, acc_ref, surv_ref = rest[blocks_per_page * (1 + pages_per_step):]
    p = pl.program_id(1)
    row = lax.broadcasted_iota(jnp.int32, (PAGE_SIZE, PAGE_SIZE), 0)
    col = lax.broadcasted_iota(jnp.int32, (PAGE_SIZE, PAGE_SIZE), 1)
    later = (row > col).astype(BF16)
    later2 = jnp.concatenate([later, later], axis=0)
    qs = [(q_ref[0, h] * SB_SCALE).astype(BF16) for h in range(H_SB)]

    def head_tile(ref, i):
        flat = ref.reshape(PAGE_SIZE * SB_BLOCK_HEADS, HD)
        return flat[pl.ds(i, PAGE_SIZE, stride=SB_BLOCK_HEADS), :].astype(BF16)

    def page(blocks, mask, surv):
        zs = []
        for h in range(H_SB):
            kh = head_tile(blocks[h // SB_BLOCK_HEADS], h % SB_BLOCK_HEADS)
            z = lax.dot_general(qs[h], kh, (((1,), (1,)), ((), ())), preferred_element_type=F32)
            zs.append(z + bias_ref[0, h])
        z = jnp.concatenate(zs, axis=0)
        t = jnp.log(1.0 + jnp.exp(-jnp.abs(z)))
        ls = jnp.minimum(z, 0.0) - t
        lneg = ls - z
        if mask is not None:
            lneg = jnp.where(mask, lneg, 0.0)
        hi, lo = _split_bf16(lneg)
        between = jnp.dot(jnp.concatenate([hi, lo], axis=1), later2, preferred_element_type=F32)
        a = jnp.exp(ls + between + surv)
        if mask is not None:
            a = jnp.where(mask, a, 0.0)
        outs = []
        v0 = H_SB // SB_BLOCK_HEADS
        for h in range(H_SB):
            vh = head_tile(blocks[v0 + h // SB_BLOCK_HEADS], h % SB_BLOCK_HEADS)
            ah = a[h * SB_Q_SLOTS:(h + 1) * SB_Q_SLOTS].astype(BF16)
            outs.append(jnp.dot(ah, vh, preferred_element_type=F32))
        return jnp.concatenate(outs, axis=0), surv + jnp.sum(lneg, axis=-1, keepdims=True)

    @pl.when(p == 0)
    def _():
        mask = (col < (row & (SB_Q_SLOTS - 1))) & (col < n_new)
        contrib, surv = page(new_refs, mask, jnp.zeros((H_SB * SB_Q_SLOTS, 1), F32))
        acc_ref[...] = contrib
        surv_ref[...] = surv

    surv = surv_ref[...]
    total = acc_ref[...]
    for j in range(pages_per_step):
        contrib, surv = page(page_refs[j * blocks_per_page:(j + 1) * blocks_per_page], None, surv)
        total = total + contrib
    acc_ref[...] = total
    surv_ref[...] = surv

    @pl.when(p == pl.num_programs(1) - 1)
    def _():
        for h in range(H_SB):
            o_ref[0, :, h * HD:(h + 1) * HD] = acc_ref[h * SB_Q_SLOTS:(h + 1) * SB_Q_SLOTS, :].astype(o_ref.dtype)


def _sb_sample(qkv, pools, layer, page_table, bias, *, n_batch, n_new, pages_per_step=8):
    hw = H_SB * HD
    n_pool = pools.shape[1]
    n_pages = page_table.shape[1]
    blocks_per_page = 2 * H_SB // SB_BLOCK_HEADS
    q = qkv[:, :hw].reshape(n_batch, n_new, H_SB, HD).transpose(0, 2, 1, 3)
    q = jnp.pad(q, ((0, 0), (0, 0), (0, SB_Q_SLOTS - n_new), (0, 0)))
    new_kv = jnp.pad(qkv[:, hw:].reshape(n_batch, n_new, 2 * H_SB, HD),
                     ((0, 0), (0, PAGE_SIZE - n_new), (0, 0), (0, 0)))
    pool4 = pools.reshape(pools.shape[0] * n_pool, PAGE_SIZE, 2 * H_SB, HD)
    blk = (1, PAGE_SIZE, SB_BLOCK_HEADS, HD)

    def page_spec(j, hb):
        return pl.BlockSpec(
            blk, lambda b, p, pt: (layer * n_pool + pt[b, n_pages - 1 - (p * pages_per_step + j)], 0, hb, 0))

    grid_spec = pltpu.PrefetchScalarGridSpec(
        num_scalar_prefetch=1,
        grid=(n_batch, n_pages // pages_per_step),
        in_specs=[pl.BlockSpec(memory_space=pltpu.SMEM),
                  pl.BlockSpec((1, H_SB, SB_Q_SLOTS, HD), lambda b, p, pt: (b, 0, 0, 0))]
        + [pl.BlockSpec(blk, lambda b, p, pt, hb=hb: (b, 0, hb, 0)) for hb in range(blocks_per_page)]
        + [page_spec(j, hb) for j in range(pages_per_step) for hb in range(blocks_per_page)],
        out_specs=pl.BlockSpec((1, SB_Q_SLOTS, hw), lambda b, p, pt: (b, 0, 0)),
        scratch_shapes=[pltpu.VMEM((H_SB * SB_Q_SLOTS, HD), F32), pltpu.VMEM((H_SB * SB_Q_SLOTS, 1), F32)],
    )
    out = pl.pallas_call(
        functools.partial(_sb_sample_kernel, n_new=n_new, pages_per_step=pages_per_step),
        grid_spec=grid_spec,
        out_shape=jax.ShapeDtypeStruct((n_batch, SB_Q_SLOTS, hw), BF16),
        compiler_params=_cparams(("parallel", "arbitrary")),
        name="sb_sample",
    )(page_table, bias.reshape(1, H_SB), q, *([new_kv] * blocks_per_page),
      *([pool4] * (pages_per_step * blocks_per_page)))
    return out[:, :n_new].reshape(n_batch * n_new, hw)


def _heads_to_rows_kernel(x_ref, o_ref):
    tm = x_ref.shape[0]
    flat = o_ref.reshape(tm * SUBLANES, HD)
    for c in range(SUBLANES):
        flat[pl.ds(c, tm, stride=SUBLANES), :] = x_ref[:, c * HD:(c + 1) * HD]


def _heads_to_rows(x, *, n_batch, length, first_block, block_stride, n_blocks):
    m, _ = x.shape
    t = m // n_batch
    tm = min(length, 512)
    nt = length // tm
    first = (t - length) // tm
    w = SUBLANES * HD
    return pl.pallas_call(
        _heads_to_rows_kernel,
        grid=(n_batch, nt, n_blocks),
        in_specs=[pl.BlockSpec((tm, w), lambda b, i, j: (b * (t // tm) + first + i, first_block + j * block_stride))],
        out_specs=pl.BlockSpec((tm, SUBLANES, HD), lambda b, i, j: (b * nt + i, j, 0)),
        out_shape=jax.ShapeDtypeStruct((n_batch * length, n_blocks * SUBLANES, HD), F32),
        compiler_params=_cparams(("parallel", "parallel", "parallel")),
        name="heads_to_rows",
    )(x)


def _tiles(m):
    return 1024 if m % 1024 == 0 else m


def _to_bf16_kernel(w_ref, o_ref):
    o_ref[...] = w_ref[...].astype(BF16)


def _to_bf16(w, *, rows):
    n_layers, k, n = w.shape
    return pl.pallas_call(
        _to_bf16_kernel,
        grid=(n_layers, k // rows),
        in_specs=[pl.BlockSpec((None, rows, n), lambda l, i: (l, i, 0))],
        out_specs=pl.BlockSpec((None, rows, n), lambda l, i: (l, i, 0)),
        out_shape=jax.ShapeDtypeStruct(w.shape, BF16),
        compiler_params=_cparams(("parallel", "parallel")),
        name="to_bf16",
    )(w)


def _ffn(x, gain, w_gate, w_up, w_down, layer):
    m = x.shape[0]
    d_ff = w_gate.shape[2]
    zeros = jnp.zeros((1, d_ff), F32)
    a = _norm_dual(x, gain, w_gate, w_up, layer, zeros, zeros, act="swiglu", col0=0, col1=0, n_out=d_ff,
                   out_dtype=BF16, tm=_tiles(m), tn=512)
    return _res_matmul(a, w_down, layer, jnp.zeros((x.shape[1],), F32), x)


def _conv_layer(x, gain, hist, w1, b1, w_dw, b_dw, ln_g, ln_b, w2, b2, layer, *, n_batch):
    m, d = x.shape
    t = m // n_batch
    b1r = b1.reshape(1, 2 * d)
    g = _norm_dual(x, gain, w1, w1, layer, b1r, b1r, act="glu", col0=0, col1=d, n_out=d, out_dtype=F32,
                   tm=_tiles(m), tn=512)
    if t >= 256:
        tt, g_pad = 256, g
    else:
        tt = 8
        g_pad = jnp.pad(g.reshape(n_batch, t, d), ((0, 0), (0, tt - t), (0, 0))).reshape(n_batch * tt, d)
    hist_pad = jnp.pad(hist, ((0, 0), (CONV_HALO - CONV_HIST, 0), (0, 0)))
    s = _conv_core(g_pad, hist_pad, w_dw, b_dw, ln_g, ln_b, n_batch=n_batch, tt=tt)
    if tt != 256:
        s = s.reshape(n_batch, tt, d)[:, :t].reshape(m, d)
    x = _res_matmul(s, w2, layer, b2, x)
    state = jnp.concatenate([hist, g.reshape(n_batch, t, d)], axis=1)[:, -CONV_HIST:]
    return x, state


def _dil_qkv(x, gain, w_qkv, layer, q_gain, k_gain):
    m = x.shape[0]
    third = w_qkv.shape[2] // 3
    return _norm_single(x, gain, w_qkv, layer, jnp.stack([q_gain, k_gain]), n_normed_cols=2 * third,
                        tm=_tiles(m), tn=QKV_TN)


def _dil_states(qkv, n_batch):
    t = qkv.shape[0] // n_batch
    q5 = qkv.reshape(n_batch, t, 3, len(DIL_GROUPS), H_DIL, HD)
    return [jnp.stack([q5[:, :, 1, g], q5[:, :, 2, g]], axis=2) for g in range(len(DIL_GROUPS))]


def kernel(x_prompt, x_sample, state_conv, cache_win_g0, cache_win_g1, cache_win_g2, cache_sb_kv, page_table, ln_mix, ln_ffn, conv_w1, conv_b1, conv_w_dw, conv_b_dw, conv_ln_g, conv_ln_b, conv_w2, conv_b2, dil_w_qkv, dil_q_gain, dil_k_gain, dil_w_o, sb_w_qkv, sb_w_o, sb_bias, ffn_w_gate, ffn_w_up, ffn_w_down):
    bp, tp, d = x_prompt.shape
    bs, ts, _ = x_sample.shape
    depth = ln_mix.shape[0]
    xp = x_prompt.reshape(bp * tp, d)
    xs = x_sample.reshape(bs * ts, d)
    win_caches = (cache_win_g0, cache_win_g1, cache_win_g2)
    conv_p, conv_s, sb_p, sb_s = [], [], [], []
    win_p = [[] for _ in DIL_GROUPS]
    win_s = [[] for _ in DIL_GROUPS]
    i_conv = i_dil = i_sb = 0
    zero_d = jnp.zeros((d,), F32)
    w_down = _to_bf16(ffn_w_down, rows=512)
    for i in range(depth):
        kind = i % 3
        if kind == 0:
            j = i_conv
            prm = (conv_w1, conv_b1[j], conv_w_dw[j], conv_b_dw[j], conv_ln_g[j], conv_ln_b[j], conv_w2, conv_b2[j], j)
            xp, st_p = _conv_layer(xp, ln_mix[i], jnp.zeros((bp, CONV_HIST, d), F32), *prm, n_batch=bp)
            xs, st_s = _conv_layer(xs, ln_mix[i], state_conv[j], *prm, n_batch=bs)
            conv_p.append(st_p)
            conv_s.append(st_s)
            i_conv += 1
        elif kind == 1:
            j = i_dil
            qkv_p = _dil_qkv(xp, ln_mix[i], dil_w_qkv, j, dil_q_gain[j], dil_k_gain[j])
            qkv_s = _dil_qkv(xs, ln_mix[i], dil_w_qkv, j, dil_q_gain[j], dil_k_gain[j])
            o_p = _dil_prompt(qkv_p, n_batch=bp)
            o_s = _dil_sample(qkv_s, win_caches, j, n_batch=bs, n_new=ts)
            xp = _res_matmul(o_p, dil_w_o, j, zero_d, xp)
            xs = _res_matmul(o_s, dil_w_o, j, zero_d, xs)
            new_s = _dil_states(qkv_s, bs)
            n_g = len(DIL_GROUPS)
            for g, (window, _) in enumerate(DIL_GROUPS):
                keep = min(window, tp)
                st = _heads_to_rows(qkv_p, n_batch=bp, length=keep, first_block=n_g + g, block_stride=n_g, n_blocks=2)
                win_p[g].append(st.reshape(bp, keep, 2, H_DIL, HD))
                length = win_caches[g].shape[2]
                win_s[g].append(jnp.concatenate([win_caches[g][j], new_s[g]], axis=1)[:, -length:])
            i_dil += 1
        else:
            j = i_sb
            hw = H_SB * HD
            ones_gain = jnp.ones((1, HD), F32)
            qkv_p = _norm_single(xp, ln_mix[i], sb_w_qkv, j, ones_gain, n_normed_cols=0, tm=_tiles(xp.shape[0]), tn=QKV_TN)
            qkv_s = _norm_single(xs, ln_mix[i], sb_w_qkv, j, ones_gain, n_normed_cols=0, tm=_tiles(xs.shape[0]), tn=QKV_TN)
            o_p, kv_p = _sb_prompt(qkv_p, sb_bias[j], n_batch=bp, qb=256)
            o_s = _sb_sample(qkv_s, cache_sb_kv, j, page_table, sb_bias[j], n_batch=bs, n_new=ts)
            xp = _res_matmul(o_p, sb_w_o, j, zero_d, xp)
            xs = _res_matmul(o_s, sb_w_o, j, zero_d, xs)
            sb_p.append(kv_p)
            sb_s.append(qkv_s[:, hw:].reshape(bs, ts, 2, H_SB, HD))
            i_sb += 1
        xp = _ffn(xp, ln_ffn[i], ffn_w_gate, ffn_w_up, w_down, i)
        xs = _ffn(xs, ln_ffn[i], ffn_w_gate, ffn_w_up, w_down, i)
    return (xp.reshape(bp, tp, d), xs.reshape(bs, ts, d), jnp.stack(conv_p), jnp.stack(conv_s),
            jnp.stack(win_p[0]), jnp.stack(win_s[0]), jnp.stack(win_p[1]), jnp.stack(win_s[1]),
            jnp.stack(win_p[2]), jnp.stack(win_s[2]), jnp.stack(sb_p), jnp.stack(sb_s))
```
